```python
import math
import jax
import jax.numpy as jnp
from jax import lax
import numpy as np

D_MODEL = 1024
BATCH = 2
SEQ = 16384
DEPTH = 2

GRID_W = 64
CTX_LEN = 256
CHUNK = 128
Q_BLOCK = 128
ROPE_BASE = 10000.0
EPS = 1e-6

BRANCH_DIM = D_MODEL // 2
N_BRANCH = 4
CONV_DIM = BRANCH_DIM
CONV_WIDTH = 31
SSM_INNER = BRANCH_DIM
SSM_HEAD_DIM = 64
SSM_HEADS = SSM_INNER // SSM_HEAD_DIM
SSM_GROUPS = 2
SSM_STATE = 128
SSM_CONV = 5
SSM_XBC = SSM_INNER + 2 * SSM_GROUPS * SSM_STATE
RET_HEADS = 4
RET_QK_DIM = 64
RET_INNER = BRANCH_DIM
RET_V_DIM = RET_INNER // RET_HEADS
MLA_HEADS = 8
MLA_NOPE = 64
MLA_ROPE = 32
MLA_V = BRANCH_DIM // MLA_HEADS
MLA_Q_RANK = 384
MLA_KV_RANK = 256
MLA_INNER = MLA_HEADS * MLA_V
FFN_DIM = ((8 * D_MODEL // 3 + 255) // 256) * 256

IN_SPLITS = (
    2 * CONV_DIM,
    SSM_INNER, SSM_XBC, 2 * SSM_HEADS,
    RET_HEADS * RET_QK_DIM, RET_HEADS * RET_QK_DIM, RET_INNER, RET_INNER,
    MLA_Q_RANK, MLA_KV_RANK, MLA_ROPE,
    N_BRANCH * D_MODEL,
)
N_IN = sum(IN_SPLITS)

kernel_name = 'hybrid_gated_parallel_dit_block'


def split_cols(u, sizes):
    parts, off = [], 0
    for s in sizes:
        parts.append(u[..., off:off + s])
        off += s
    return parts


def flip(a):
    return jnp.flip(a, axis=1)


def rmsnorm(x, g):
    xf = x.astype(jnp.float32)
    y = xf * lax.rsqrt(jnp.mean(xf * xf, axis=-1, keepdims=True) + EPS)
    return y.astype(x.dtype) * g


def layernorm(x, g, b):
    xf = x.astype(jnp.float32)
    mu = jnp.mean(xf, axis=-1, keepdims=True)
    var = jnp.mean(jnp.square(xf - mu), axis=-1, keepdims=True)
    return ((xf - mu) * lax.rsqrt(var + EPS)).astype(x.dtype) * g + b


def modulate(h, shift, scale):
    return h * (1 + scale) + shift


def axial_rope(n, rot_dim, dtype):
    rows = n // GRID_W
    row = jnp.repeat(jnp.arange(rows, dtype=jnp.float32), GRID_W)
    col = jnp.tile(jnp.arange(GRID_W, dtype=jnp.float32), rows)
    nf = rot_dim // 4
    inv = ROPE_BASE ** (-jnp.arange(nf, dtype=jnp.float32) / nf)
    ang = jnp.concatenate([row[:, None] * inv, col[:, None] * inv], axis=-1)
    return jnp.cos(ang).astype(dtype), jnp.sin(ang).astype(dtype)


def apply_rope(x, rope):
    if rope is None:
        return x
    cos, sin = rope[0][:, None, :], rope[1][:, None, :]
    x1, x2 = jnp.split(x, 2, axis=-1)
    return jnp.concatenate([x1 * cos - x2 * sin, x1 * sin + x2 * cos], axis=-1)


def dwconv(x, w, b):
    pad = w.shape[0] // 2
    y = lax.conv_general_dilated(x, w[:, None, :].astype(x.dtype), window_strides=(1,),
                                 padding=((pad, pad),), dimension_numbers=('NWC', 'WIO', 'NWC'),
                                 feature_group_count=x.shape[-1])
    return y + b


def chunked_scan(q, k, v, log_a, h0, need_y):
    f32 = jnp.float32
    b, l, nh, n = k.shape
    p = v.shape[-1]
    nc = l // CHUNK
    k = k.astype(f32).reshape(b, nc, CHUNK, nh, n)
    v = v.astype(f32).reshape(b, nc, CHUNK, nh, p)
    cum = jnp.cumsum(log_a.astype(f32).reshape(b, nc, CHUNK, nh), axis=2)
    total = cum[:, :, -1]
    s_local = jnp.einsum('bclhn,bclhp->bchnp', k * jnp.exp(total[:, :, None] - cum)[..., None], v)

    def step(h, inp):
        s_c, t_c = inp
        return jnp.exp(t_c)[..., None, None] * h + s_c, h

    h_last, h_enter = lax.scan(step, h0.astype(f32), (jnp.moveaxis(s_local, 1, 0), jnp.moveaxis(total, 1, 0)))
    if not need_y:
        return None, h_last
    q = q.astype(f32).reshape(b, nc, CHUNK, nh, n)
    h_enter = jnp.moveaxis(h_enter, 0, 1)
    seg = cum[:, :, :, None, :] - cum[:, :, None, :, :]
    mask = jnp.tril(jnp.ones((CHUNK, CHUNK), dtype=bool))[None, None, :, :, None]
    decay = jnp.exp(jnp.where(mask, seg, -jnp.inf))
    scores = jnp.einsum('bclhn,bcshn->bclsh', q, k) * decay
    y = (jnp.einsum('bclsh,bcshp->bclhp', scores, v)
         + jnp.einsum('bclhn,bchnp->bclhp', q, h_enter) * jnp.exp(cum)[..., None])
    return y.reshape(b, l, nh, p), h_last


def conv_module(u, p):
    a, g = jnp.split(u, 2, axis=-1)
    h = dwconv(a * jax.nn.sigmoid(g), p['conv_w'], p['conv_b'])
    return jax.nn.silu(layernorm(h, p['conv_ln_g'], p['conv_ln_b']))


def ssm_mixer(z, xbc, dt_raw, p, h0, need_y):
    f32 = jnp.float32
    b, l, _ = xbc.shape
    xbc = jax.nn.silu(dwconv(xbc, p['ssm_conv_w'], p['ssm_conv_b']))
    xh, bm, cm = split_cols(xbc, (SSM_INNER, SSM_GROUPS * SSM_STATE, SSM_GROUPS * SSM_STATE))
    xf = xh.reshape(b, l, SSM_HEADS, SSM_HEAD_DIM).astype(f32)
    rep = SSM_HEADS // SSM_GROUPS
    bm = jnp.repeat(bm.reshape(b, l, SSM_GROUPS, SSM_STATE), rep, axis=2)
    cm = jnp.repeat(cm.reshape(b, l, SSM_GROUPS, SSM_STATE), rep, axis=2)
    dt = jax.nn.softplus(dt_raw.astype(f32).reshape(b, l, 2, SSM_HEADS) + p['ssm_dt_bias'].astype(f32))
    log_a = dt * -jnp.exp(p['ssm_a_log'].astype(f32))
    y_f, h_f = chunked_scan(cm, bm, xf * dt[:, :, 0, :, None], log_a[:, :, 0], h0[0], need_y)
    y_b, h_b = chunked_scan(flip(cm), flip(bm), flip(xf * dt[:, :, 1, :, None]), flip(log_a[:, :, 1]), h0[1], need_y)
    if not need_y:
        return None, (h_f, h_b)
    y = y_f + flip(y_b) + p['ssm_d'].astype(f32)[:, None] * xf
    y = y.reshape(b, l, SSM_INNER).astype(z.dtype) * jax.nn.silu(z)
    y = rmsnorm(y.reshape(b, l, SSM_GROUPS, SSM_INNER // SSM_GROUPS), p['ssm_norm_g'].reshape(SSM_GROUPS, -1))
    return y.reshape(b, l, SSM_INNER), (h_f, h_b)


def retention_mixer(q, k, v, g, p, h0, rope, need_y):
    b, l, _ = k.shape
    q = apply_rope(q.reshape(b, l, RET_HEADS, RET_QK_DIM), rope)
    k = apply_rope(k.reshape(b, l, RET_HEADS, RET_QK_DIM), rope) * (RET_QK_DIM ** -0.5)
    v = v.reshape(b, l, RET_HEADS, RET_V_DIM)
    log_gamma = -jnp.exp(p['ret_decay'].astype(jnp.float32))
    la_f = jnp.broadcast_to(log_gamma[0], (b, l, RET_HEADS))
    la_b = jnp.broadcast_to(log_gamma[1], (b, l, RET_HEADS))
    y_f, h_f = chunked_scan(q, k, v, la_f, h0[0], need_y)
    y_b, h_b = chunked_scan(flip(q), flip(k), flip(v), la_b, h0[1], need_y)
    if not need_y:
        return None, (h_f, h_b)
    y = (y_f + flip(y_b)).astype(g.dtype)
    yn = layernorm(y, p['ret_gn_g'].reshape(RET_HEADS, RET_V_DIM), p['ret_gn_b'].reshape(RET_HEADS, RET_V_DIM))
    return jax.nn.silu(g) * yn.reshape(b, l, RET_INNER), (h_f, h_b)


def mla_kv(c_kv, k_r, p, rope):
    b, l, _ = c_kv.shape
    kv = (rmsnorm(c_kv, p['mla_kv_norm_g']) @ p['mla_w_ukv']).reshape(b, l, MLA_HEADS, MLA_NOPE + MLA_V)
    k_rope = apply_rope(k_r[:, :, None, :], rope)
    k = jnp.concatenate([kv[..., :MLA_NOPE], jnp.broadcast_to(k_rope, (b, l, MLA_HEADS, MLA_ROPE))], axis=-1)
    return k, kv[..., MLA_NOPE:]


def mla_q(c_q, p, rope):
    b, l, _ = c_q.shape
    q = (rmsnorm(c_q, p['mla_q_norm_g']) @ p['mla_w_uq']).reshape(b, l, MLA_HEADS, MLA_NOPE + MLA_ROPE)
    return jnp.concatenate([q[..., :MLA_NOPE], apply_rope(q[..., MLA_NOPE:], rope)], axis=-1)


def block_attention(q, k, v):
    b, lq, h, d = q.shape
    nb = lq // Q_BLOCK
    scale = d ** -0.5
    qb = jnp.moveaxis(q.reshape(b, nb, Q_BLOCK, h, d), 1, 0)

    def one(qi):
        s = jnp.einsum('bqhd,bkhd->bhqk', qi, k).astype(jnp.float32) * scale
        w = jax.nn.softmax(s, axis=-1).astype(v.dtype)
        return jnp.einsum('bhqk,bkhd->bqhd', w, v)

    o = lax.map(one, qb)
    return jnp.moveaxis(o, 0, 1).reshape(b, lq, h * v.shape[-1])


def merge_branches(branches, gate_logits, p):
    merged = jax.nn.sigmoid(gate_logits[..., :D_MODEL]) * (branches[0] @ p['w_branch'][0])
    for i in range(1, N_BRANCH):
        gi = jax.nn.sigmoid(gate_logits[..., i * D_MODEL:(i + 1) * D_MODEL])
        merged = merged + gi * (branches[i] @ p['w_branch'][i])
    return merged @ p['w_out']


def token_mixers(hx, hc, p, rope_ret, rope_mla, need_ctx_out):
    f32 = jnp.float32
    b = hx.shape[0]
    (conv_x, z_x, xbc_x, dt_x, rq_x, rk_x, rv_x, rg_x, cq_x, ckv_x, kr_x, gl_x) = split_cols(hx @ p['w_in'], IN_SPLITS)
    (conv_c, z_c, xbc_c, dt_c, rq_c, rk_c, rv_c, rg_c, cq_c, ckv_c, kr_c, gl_c) = split_cols(hc @ p['w_in'], IN_SPLITS)
    zs = jnp.zeros((b, SSM_HEADS, SSM_STATE, SSM_HEAD_DIM), f32)
    zr = jnp.zeros((b, RET_HEADS, RET_QK_DIM, RET_V_DIM), f32)
    ssm_yc, ssm_hc = ssm_mixer(z_c, xbc_c, dt_c, p, (zs, zs), need_ctx_out)
    ret_yc, ret_hc = retention_mixer(rq_c, rk_c, rv_c, rg_c, p, (zr, zr), None, need_ctx_out)
    k_c, v_c = mla_kv(ckv_c, kr_c, p, None)
    ssm_yx, _ = ssm_mixer(z_x, xbc_x, dt_x, p, ssm_hc, True)
    ret_yx, _ = retention_mixer(rq_x, rk_x, rv_x, rg_x, p, ret_hc, rope_ret, True)
    k_x, v_x = mla_kv(ckv_x, kr_x, p, rope_mla)
    att_x = block_attention(mla_q(cq_x, p, rope_mla), jnp.concatenate([k_c, k_x], axis=1),
                            jnp.concatenate([v_c, v_x], axis=1))
    out_x = merge_branches((conv_module(conv_x, p), ssm_yx, ret_yx, att_x), gl_x, p)
    if not need_ctx_out:
        return out_x, None
    att_c = block_attention(mla_q(cq_c, p, None), k_c, v_c)
    out_c = merge_branches((conv_module(conv_c, p), ssm_yc, ret_yc, att_c), gl_c, p)
    return out_x, out_c


def swiglu(h, p):
    a, g = jnp.split(h @ p['w_ffn_in'], 2, axis=-1)
    return (jax.nn.silu(g) * a) @ p['w_ffn_out']


def trunk_layer(xs, cs, c, c_ctx, p, rope_ret, rope_mla, last):
    mod_x = jnp.split((jax.nn.silu(c) @ p['w_ada'] + p['b_ada'])[:, None, :], 6, axis=-1)
    mod_c = jnp.split((jax.nn.silu(c_ctx) @ p['w_ada'] + p['b_ada'])[None, None, :], 6, axis=-1)
    hx = modulate(rmsnorm(xs, p['norm1_g']), mod_x[0], mod_x[1])
    hc = modulate(rmsnorm(cs, p['norm1_g']), mod_c[0], mod_c[1])
    mx, mc = token_mixers(hx, hc, p, rope_ret, rope_mla, not last)
    xs = xs + mod_x[2] * mx
    xs = xs + mod_x[5] * swiglu(modulate(rmsnorm(xs, p['norm2_g']), mod_x[3], mod_x[4]), p)
    if last:
        return xs, cs
    cs = cs + mod_c[2] * mc
    cs = cs + mod_c[5] * swiglu(modulate(rmsnorm(cs, p['norm2_g']), mod_c[3], mod_c[4]), p)
    return xs, cs


def setup_inputs(seed: int = 0) -> dict:
    key = jax.random.key(seed)
    k = jax.random.split(key, 32)
    f32 = jnp.float32
    L, D = DEPTH, D_MODEL

    def nrm(i, shape, scale):
        return jax.random.normal(k[i], shape, f32) * scale

    def gain(i, shape):
        return 1.0 + nrm(i, shape, 0.02)

    dt0 = jnp.exp(jax.random.uniform(k[15], (L, 2, SSM_HEADS), f32, math.log(1e-3), math.log(1e-1)))
    gamma = 1.0 - 2.0 ** (-5.0 - jnp.arange(RET_HEADS, dtype=f32))
    return {
        'x': nrm(0, (BATCH, SEQ, D), 1.0),
        'c': nrm(1, (BATCH, D), 1.0),
        'ctx': nrm(2, (BATCH, CTX_LEN, D), 1.0),
        'c_ctx': nrm(3, (D,), 1.0),
        'w_ada': nrm(4, (L, D, 6 * D), 0.5 * D ** -0.5),
        'b_ada': nrm(5, (L, 6 * D), 0.02),
        'norm1_g': gain(6, (L, D)),
        'norm2_g': gain(7, (L, D)),
        'w_in': nrm(8, (L, D, N_IN), D ** -0.5),
        'conv_w': nrm(9, (L, CONV_WIDTH, CONV_DIM), CONV_WIDTH ** -0.5),
        'conv_b': nrm(10, (L, CONV_DIM), 0.02),
        'conv_ln_g': gain(11, (L, CONV_DIM)),
        'conv_ln_b': nrm(12, (L, CONV_DIM), 0.02),
        'ssm_conv_w': nrm(13, (L, SSM_CONV, SSM_XBC), SSM_CONV ** -0.5),
        'ssm_conv_b': nrm(14, (L, SSM_XBC), 0.02),
        'ssm_dt_bias': dt0 + jnp.log(-jnp.expm1(-dt0)),
        'ssm_a_log': jnp.log(jax.random.uniform(k[16], (L, 2, SSM_HEADS), f32, 1.0, 16.0)),
        'ssm_d': gain(17, (L, SSM_HEADS)),
        'ssm_norm_g': gain(18, (L, SSM_INNER)),
        'ret_decay': jnp.log(-jnp.log(gamma)) + nrm(19, (L, 2, RET_HEADS), 0.05),
        'ret_gn_g': gain(20, (L, RET_INNER)),
        'ret_gn_b': nrm(21, (L, RET_INNER), 0.02),
        'mla_q_norm_g': gain(22, (L, MLA_Q_RANK)),
        'mla_kv_norm_g': gain(23, (L, MLA_KV_RANK)),
        'mla_w_uq': nrm(24, (L, MLA_Q_RANK, MLA_HEADS * (MLA_NOPE + MLA_ROPE)), MLA_Q_RANK ** -0.5),
        'mla_w_ukv': nrm(25, (L, MLA_KV_RANK, MLA_HEADS * (MLA_NOPE + MLA_V)), MLA_KV_RANK ** -0.5),
        'w_branch': nrm(26, (L, N_BRANCH, BRANCH_DIM, D), BRANCH_DIM ** -0.5),
        'w_out': nrm(27, (L, D, D), D ** -0.5),
        'w_ffn_in': nrm(28, (L, D, 2 * FFN_DIM), D ** -0.5),
        'w_ffn_out': nrm(29, (L, FFN_DIM, D), FFN_DIM ** -0.5),
        'final_norm_g': gain(30, (D,)),
    }


def reference(x, c, ctx, c_ctx, w_ada, b_ada, norm1_g, norm2_g, w_in, conv_w, conv_b, conv_ln_g, conv_ln_b,
              ssm_conv_w, ssm_conv_b, ssm_dt_bias, ssm_a_log, ssm_d, ssm_norm_g, ret_decay, ret_gn_g, ret_gn_b,
              mla_q_norm_g, mla_kv_norm_g, mla_w_uq, mla_w_ukv, w_branch, w_out, w_ffn_in, w_ffn_out, final_norm_g):
    n_lat = x.shape[1]
    rope_ret = axial_rope(n_lat, RET_QK_DIM, x.dtype)
    rope_mla = axial_rope(n_lat, MLA_ROPE, x.dtype)
    xs, cs = x, ctx
    for i in range(DEPTH):
        p = {
            'w_ada': w_ada[i], 'b_ada': b_ada[i], 'norm1_g': norm1_g[i], 'norm2_g': norm2_g[i],
            'w_in': w_in[i], 'conv_w': conv_w[i], 'conv_b': conv_b[i], 'conv_ln_g': conv_ln_g[i],
            'conv_ln_b': conv_ln_b[i], 'ssm_conv_w': ssm_conv_w[i], 'ssm_conv_b': ssm_conv_b[i],
            'ssm_dt_bias': ssm_dt_bias[i], 'ssm_a_log': ssm_a_log[i], 'ssm_d': ssm_d[i],
            'ssm_norm_g': ssm_norm_g[i], 'ret_decay': ret_decay[i], 'ret_gn_g': ret_gn_g[i],
            'ret_gn_b': ret_gn_b[i], 'mla_q_norm_g': mla_q_norm_g[i], 'mla_kv_norm_g': mla_kv_norm_g[i],
            'mla_w_uq': mla_w_uq[i], 'mla_w_ukv': mla_w_ukv[i], 'w_branch': w_branch[i], 'w_out': w_out[i],
            'w_ffn_in': w_ffn_in[i], 'w_ffn_out': w_ffn_out[i],
        }
        xs, cs = trunk_layer(xs, cs, c, c_ctx, p, rope_ret, rope_mla, i == DEPTH - 1)
    return rmsnorm(xs, final_norm_g)
```

```python
import functools
import math

import jax
import jax.numpy as jnp
from jax import lax
from jax.experimental import pallas as pl
from jax.experimental.pallas import tpu as pltpu

F32 = jnp.float32
BF16 = jnp.bfloat16
HIGHEST = lax.Precision.HIGHEST

D_MODEL = 1024
GRID_W = 64
CHUNK = 128
ROPE_BASE = 10000.0
EPS = 1e-6
BRANCH_DIM = D_MODEL // 2
N_BRANCH = 4
CONV_DIM = BRANCH_DIM
CONV_WIDTH = 31
SSM_INNER = BRANCH_DIM
SSM_HEAD_DIM = 64
SSM_HEADS = SSM_INNER // SSM_HEAD_DIM
SSM_GROUPS = 2
SSM_STATE = 128
SSM_CONV = 5
SSM_XBC = SSM_INNER + 2 * SSM_GROUPS * SSM_STATE
RET_HEADS = 4
RET_QK_DIM = 64
RET_INNER = BRANCH_DIM
RET_V_DIM = RET_INNER // RET_HEADS
MLA_HEADS = 8
MLA_NOPE = 64
MLA_ROPE = 32
MLA_V = BRANCH_DIM // MLA_HEADS
MLA_Q_RANK = 384
MLA_KV_RANK = 256
FFN_DIM = ((8 * D_MODEL // 3 + 255) // 256) * 256

LANES = 128
VMEM_LIMIT_BYTES = 56 * 1024 * 1024

COL_GL = 0
COL_CONV = 4096
COL_XBC = 5120
COL_Z = 6144
COL_RV = 6656
COL_RG = 7168
COL_RQ = 7680
COL_RK = 7936
COL_CKV = 8192
COL_CQ = 8448
COL_DT = 8832
COL_KR = 8960
N_IN_PAD = 9216


def _tile(n, target, mult):
    best = None
    for t in range(mult, min(n, target) + 1, mult):
        if n % t == 0:
            best = t
    assert best is not None, (n, target, mult)
    return best


def _cparams(sem):
    return pltpu.CompilerParams(dimension_semantics=sem, vmem_limit_bytes=VMEM_LIMIT_BYTES)


def _silu(v):
    return v * jax.nn.sigmoid(v)


def _norm_mod(x, mod, g, row0, n_ctx, k):
    y = x * lax.rsqrt(jnp.mean(x * x, axis=-1, keepdims=True) + EPS) * g
    rows = row0 + lax.broadcasted_iota(jnp.int32, (x.shape[0], 1), 0)
    is_ctx = rows < n_ctx
    shift = jnp.where(is_ctx, mod[3 * k:3 * k + 1], mod[8 + 3 * k:9 + 3 * k])
    scale = jnp.where(is_ctx, mod[3 * k + 1:3 * k + 2], mod[9 + 3 * k:10 + 3 * k])
    return y * (1.0 + scale) + shift


def _res_gate(mod, row0, n_rows, n_ctx, k):
    rows = row0 + lax.broadcasted_iota(jnp.int32, (n_rows, 1), 0)
    return jnp.where(rows < n_ctx, mod[3 * k + 2:3 * k + 3], mod[10 + 3 * k:11 + 3 * k])


def _ada_kernel(c_ref, w_ref, b_ref, o_ref):
    o_ref[...] = jnp.dot(_silu(c_ref[...]), w_ref[...], preferred_element_type=F32, precision=HIGHEST) + b_ref[...]


def _ada(cvec, w, b):
    n = w.shape[1]
    tn = _tile(n, 1536, LANES)
    return pl.pallas_call(
        _ada_kernel,
        out_shape=jax.ShapeDtypeStruct((cvec.shape[0], n), F32),
        grid=(n // tn,),
        in_specs=[pl.BlockSpec(cvec.shape, lambda j: (0, 0)),
                  pl.BlockSpec((w.shape[0], tn), lambda j: (0, j)),
                  pl.BlockSpec((1, tn), lambda j: (0, j))],
        out_specs=pl.BlockSpec((cvec.shape[0], tn), lambda j: (0, j)),
        compiler_params=_cparams(("arbitrary",)),
    )(cvec, w, b.reshape(1, n))


def _inproj_kernel(x_ref, mod_ref, g_ref, w_ref, o_ref, h_ref, *, tm, n_ctx):
    i = pl.program_id(1)

    @pl.when(pl.program_id(2) == 0)
    def _():
        h_ref[...] = _norm_mod(x_ref[0], mod_ref[0], g_ref[...], i * tm, n_ctx, 0).astype(BF16)

    o_ref[0] = jnp.dot(h_ref[...], w_ref[...], preferred_element_type=F32)


def _inproj(xs, modtab, g, w, n_ctx):
    b, nt, d = xs.shape
    n = w.shape[1]
    tm = _tile(nt, 1280, 256)
    tn = _tile(n, 1024, LANES)
    return pl.pallas_call(
        functools.partial(_inproj_kernel, tm=tm, n_ctx=n_ctx),
        out_shape=jax.ShapeDtypeStruct((b, nt, n), F32),
        grid=(b, nt // tm, n // tn),
        in_specs=[pl.BlockSpec((1, tm, d), lambda bi, i, j: (bi, i, 0)),
                  pl.BlockSpec((1, 16, d), lambda bi, i, j: (bi, 0, 0)),
                  pl.BlockSpec((1, d), lambda bi, i, j: (0, 0)),
                  pl.BlockSpec((d, tn), lambda bi, i, j: (0, j))],
        out_specs=pl.BlockSpec((1, tm, tn), lambda bi, i, j: (bi, i, j)),
        scratch_shapes=[pltpu.VMEM((tm, d), BF16)],
        compiler_params=_cparams(("parallel", "parallel", "arbitrary")),
    )(xs, modtab, g.reshape(1, d), w)


def _halo_flags(i, t, n_ctx, n_tiles):
    zero_prev = jnp.logical_or(i == 0, i * t == n_ctx)
    zero_next = jnp.logical_or((i + 1) * t == n_ctx, i == n_tiles - 1)
    return zero_prev, zero_next


def _convmod_kernel(cur_ref, prev_ref, next_ref, w_ref, b_ref, lg_ref, lb_ref, o_ref, buf_ref, acc_ref,
                    *, t, n_ctx, n_tiles):
    i = pl.program_id(1)
    zero_prev, zero_next = _halo_flags(i, t, n_ctx, n_tiles)

    def glu(u):
        return u[:, :CONV_DIM] * jax.nn.sigmoid(u[:, CONV_DIM:])

    halo = 16
    buf_ref[0:halo, :] = jnp.where(zero_prev, 0.0, glu(prev_ref[0]))
    buf_ref[halo:halo + t, :] = glu(cur_ref[0])
    buf_ref[halo + t:2 * halo + t, :] = jnp.where(zero_next, 0.0, glu(next_ref[0]))
    pad = CONV_WIDTH // 2
    rc = 64
    for r in range(t // rc):
        for c in range(CONV_DIM // LANES):
            cs = slice(c * LANES, (c + 1) * LANES)
            acc = jnp.zeros((rc, LANES), F32)
            for k in range(CONV_WIDTH):
                off = halo + r * rc + k - pad
                acc = acc + w_ref[k:k + 1, cs] * buf_ref[off:off + rc, cs]
            acc_ref[r * rc:(r + 1) * rc, cs] = acc + b_ref[:, cs]
    h = acc_ref[...]
    mu = jnp.mean(h, axis=-1, keepdims=True)
    var = jnp.mean(jnp.square(h - mu), axis=-1, keepdims=True)
    y = (h - mu) * lax.rsqrt(var + EPS) * lg_ref[...] + lb_ref[...]
    o_ref[0] = _silu(y)


def _convmod(u, w, bias, ln_g, ln_b, n_ctx):
    b, nt, _ = u.shape
    t = 256
    n_tiles = nt // t
    hb = t // 16
    last_hb = nt // 16 - 1
    width = 2 * CONV_DIM
    cb = COL_CONV // width
    return pl.pallas_call(
        functools.partial(_convmod_kernel, t=t, n_ctx=n_ctx, n_tiles=n_tiles),
        out_shape=jax.ShapeDtypeStruct((b, nt, CONV_DIM), F32),
        grid=(b, n_tiles),
        in_specs=[pl.BlockSpec((1, t, width), lambda bi, i: (bi, i, cb)),
                  pl.BlockSpec((1, 16, width), lambda bi, i: (bi, jnp.maximum(i * hb - 1, 0), cb)),
                  pl.BlockSpec((1, 16, width), lambda bi, i: (bi, jnp.minimum((i + 1) * hb, last_hb), cb)),
                  pl.BlockSpec((CONV_WIDTH, CONV_DIM), lambda bi, i: (0, 0)),
                  pl.BlockSpec((1, CONV_DIM), lambda bi, i: (0, 0)),
                  pl.BlockSpec((1, CONV_DIM), lambda bi, i: (0, 0)),
                  pl.BlockSpec((1, CONV_DIM), lambda bi, i: (0, 0))],
        out_specs=pl.BlockSpec((1, t, CONV_DIM), lambda bi, i: (bi, i, 0)),
        scratch_shapes=[pltpu.VMEM((t + 32, CONV_DIM), F32), pltpu.VMEM((t, CONV_DIM), F32)],
        compiler_params=_cparams(("parallel", "parallel")),
    )(u, u, u, w, bias.reshape(1, -1), ln_g.reshape(1, -1), ln_b.reshape(1, -1))


def _ssmconv_kernel(cur_ref, prev_ref, next_ref, w_ref, b_ref, o_ref, buf_ref, *, t, n_ctx, n_tiles):
    i = pl.program_id(1)
    zero_prev, zero_next = _halo_flags(i, t, n_ctx, n_tiles)
    halo = 8
    buf_ref[0:halo, :] = jnp.where(zero_prev, 0.0, prev_ref[0])
    buf_ref[halo:halo + t, :] = cur_ref[0]
    buf_ref[halo + t:2 * halo + t, :] = jnp.where(zero_next, 0.0, next_ref[0])
    pad = SSM_CONV // 2
    rc = 64
    for r in range(t // rc):
        for c in range(SSM_XBC // LANES):
            cs = slice(c * LANES, (c + 1) * LANES)
            acc = jnp.zeros((rc, LANES), F32)
            for k in range(SSM_CONV):
                off = halo + r * rc + k - pad
                acc = acc + w_ref[k:k + 1, cs] * buf_ref[off:off + rc, cs]
            o_ref[0, r * rc:(r + 1) * rc, cs] = _silu(acc + b_ref[:, cs])


def _ssmconv(u, w, bias, n_ctx):
    b, nt, _ = u.shape
    t = 256
    n_tiles = nt // t
    hb = t // 8
    last_hb = nt // 8 - 1
    cb = COL_XBC // SSM_XBC
    return pl.pallas_call(
        functools.partial(_ssmconv_kernel, t=t, n_ctx=n_ctx, n_tiles=n_tiles),
        out_shape=jax.ShapeDtypeStruct((b, nt, SSM_XBC), F32),
        grid=(b, n_tiles),
        in_specs=[pl.BlockSpec((1, t, SSM_XBC), lambda bi, i: (bi, i, cb)),
                  pl.BlockSpec((1, 8, SSM_XBC), lambda bi, i: (bi, jnp.maximum(i * hb - 1, 0), cb)),
                  pl.BlockSpec((1, 8, SSM_XBC), lambda bi, i: (bi, jnp.minimum((i + 1) * hb, last_hb), cb)),
                  pl.BlockSpec((SSM_CONV, SSM_XBC), lambda bi, i: (0, 0)),
                  pl.BlockSpec((1, SSM_XBC), lambda bi, i: (0, 0))],
        out_specs=pl.BlockSpec((1, t, SSM_XBC), lambda bi, i: (bi, i, 0)),
        scratch_shapes=[pltpu.VMEM((t + 16, SSM_XBC), F32)],
        compiler_params=_cparams(("parallel", "parallel")),
    )(u, u, u, w, bias.reshape(1, -1))


def _chunk_index(s, reverse, n_chunks, n_ctx_chunks):
    if not reverse:
        return s
    return jnp.where(s < n_ctx_chunks, n_ctx_chunks - 1 - s, n_chunks + n_ctx_chunks - 1 - s)


def _softplus(v):
    return jnp.maximum(v, 0.0) + jnp.log(1.0 + jnp.exp(-jnp.abs(v)))


def _ssd_kernel(*refs, reverse):
    if reverse:
        xa_ref, dt_ref, dtb_ref, alog_ref, yf_ref, z_ref, dskip_ref, ng_ref, o_ref, h_ref = refs
    else:
        xa_ref, dt_ref, dtb_ref, alog_ref, o_ref, h_ref = refs

    @pl.when(pl.program_id(1) == 0)
    def _():
        h_ref[...] = jnp.zeros_like(h_ref)

    L = CHUNK
    xa = xa_ref[0]
    dt = _softplus(dt_ref[0] + dtb_ref[...])
    la = dt * (-jnp.exp(alog_ref[...]))
    r = lax.broadcasted_iota(jnp.int32, (L, L), 0)
    c = lax.broadcasted_iota(jnp.int32, (L, L), 1)
    keep = (r <= c) if reverse else (r >= c)
    cum = jnp.dot(keep.astype(F32), la, preferred_element_type=F32, precision=HIGHEST)
    cum_t = cum.T
    off = SSM_HEADS if reverse else 0
    tot_row = 0 if reverse else L - 1
    hpg = SSM_HEADS // SSM_GROUPS
    gw = hpg * SSM_HEAD_DIM
    lane_head = lax.broadcasted_iota(jnp.int32, (L, gw), 1) // SSM_HEAD_DIM
    ys = []
    for g in range(SSM_GROUPS):
        b_f = xa[:, SSM_INNER + g * SSM_STATE:SSM_INNER + (g + 1) * SSM_STATE]
        c_b = xa[:, SSM_INNER + (SSM_GROUPS + g) * SSM_STATE:SSM_INNER + (SSM_GROUPS + g + 1) * SSM_STATE].astype(BF16)
        gram = lax.dot_general(c_b, b_f.astype(BF16), (((1,), (1,)), ((), ())), preferred_element_type=F32)
        b_t = b_f.T.astype(BF16)

        def expand(mat):
            out = jnp.broadcast_to(mat[:, off + g * hpg + hpg - 1:off + g * hpg + hpg], (L, gw))
            for j in range(hpg - 2, -1, -1):
                out = jnp.where(lane_head == j, mat[:, off + g * hpg + j:off + g * hpg + j + 1], out)
            return out

        dt_l = expand(dt)
        cum_l = expand(cum)
        tot_l = cum_l[tot_row:tot_row + 1, :]
        xdt = xa[:, g * gw:(g + 1) * gw] * dt_l
        xdt_b = xdt.astype(BF16)
        ms = []
        for j in range(hpg):
            hh = off + g * hpg + j
            seg = cum[:, hh:hh + 1] - cum_t[hh:hh + 1, :]
            dec = jnp.exp(jnp.where(keep, seg, -jnp.inf))
            ms.append((gram * dec).astype(BF16))
        rr = jnp.dot(jnp.concatenate(ms, axis=0), xdt_b, preferred_element_type=F32)
        y = jnp.where(lane_head == 0, rr[0:L], 0.0)
        for j in range(1, hpg):
            y = y + jnp.where(lane_head == j, rr[j * L:(j + 1) * L], 0.0)
        hg = h_ref[g]
        y = y + jnp.dot(c_b, hg.astype(BF16), preferred_element_type=F32) * jnp.exp(cum_l)
        wm = jnp.exp(tot_l - cum_l)
        h_ref[g] = jnp.exp(tot_l) * hg + jnp.dot(b_t, (xdt * wm).astype(BF16), preferred_element_type=F32)
        ys.append(y)
    y = jnp.concatenate(ys, axis=1)
    if not reverse:
        o_ref[0] = y
        return
    y = yf_ref[0] + y + dskip_ref[...] * xa[:, :SSM_INNER]
    y = y * _silu(z_ref[0])
    outs = []
    for g in range(SSM_GROUPS):
        yy = y[:, g * gw:(g + 1) * gw]
        outs.append(yy * lax.rsqrt(jnp.mean(yy * yy, axis=-1, keepdims=True) + EPS) * ng_ref[:, g * gw:(g + 1) * gw])
    o_ref[0] = jnp.concatenate(outs, axis=1)


def _ssd(xa, u, dt_bias, a_log, yf, d_skip, norm_g, n_ctx, reverse):
    b, nt, _ = xa.shape
    nc = nt // CHUNK
    ncc = n_ctx // CHUNK
    cidx = functools.partial(_chunk_index, reverse=reverse, n_chunks=nc, n_ctx_chunks=ncc)
    pad16 = lambda p: jnp.pad(p.reshape(1, -1), ((0, 0), (0, LANES - 2 * SSM_HEADS)))
    in_specs = [pl.BlockSpec((1, CHUNK, SSM_XBC), lambda bi, s: (bi, cidx(s), 0)),
                pl.BlockSpec((1, CHUNK, LANES), lambda bi, s: (bi, cidx(s), COL_DT // LANES)),
                pl.BlockSpec((1, LANES), lambda bi, s: (0, 0)),
                pl.BlockSpec((1, LANES), lambda bi, s: (0, 0))]
    args = [xa, u, pad16(dt_bias), pad16(a_log)]
    if reverse:
        in_specs += [pl.BlockSpec((1, CHUNK, SSM_INNER), lambda bi, s: (bi, cidx(s), 0)),
                     pl.BlockSpec((1, CHUNK, SSM_INNER), lambda bi, s: (bi, cidx(s), COL_Z // SSM_INNER)),
                     pl.BlockSpec((1, SSM_INNER), lambda bi, s: (0, 0)),
                     pl.BlockSpec((1, SSM_INNER), lambda bi, s: (0, 0))]
        args += [yf, u, jnp.repeat(d_skip, SSM_HEAD_DIM).reshape(1, -1), norm_g.reshape(1, -1)]
    return pl.pallas_call(
        functools.partial(_ssd_kernel, reverse=reverse),
        out_shape=jax.ShapeDtypeStruct((b, nt, SSM_INNER), F32),
        grid=(b, nc),
        in_specs=in_specs,
        out_specs=pl.BlockSpec((1, CHUNK, SSM_INNER), lambda bi, s: (bi, cidx(s), 0)),
        scratch_shapes=[pltpu.VMEM((SSM_GROUPS, SSM_STATE, SSM_INNER // SSM_GROUPS), F32)],
        compiler_params=_cparams(("parallel", "arbitrary")),
    )(*args)


def _ret_rope(v, cos, sin):
    lo, hi = v[:, :LANES], v[:, LANES:]
    return jnp.concatenate([lo * cos - hi * sin, lo * sin + hi * cos], axis=1)


def _ret_kernel(*refs, reverse):
    if reverse:
        (q_ref, k_ref, v_ref, cos_ref, sin_ref, dec_ref, yf_ref, g_ref, gng_ref, gnb_ref, o_ref, h_ref) = refs
    else:
        (q_ref, k_ref, v_ref, cos_ref, sin_ref, dec_ref, decb_ref, o_ref, h_ref) = refs

    @pl.when(pl.program_id(1) == 0)
    def _():
        h_ref[...] = jnp.zeros_like(h_ref)

    L = CHUNK
    qk = RET_HEADS * RET_QK_DIM
    cos, sin = cos_ref[...], sin_ref[...]
    q = _ret_rope(q_ref[0], cos, sin)
    k = _ret_rope(k_ref[0], cos, sin) * (RET_QK_DIM ** -0.5)
    v = v_ref[0]
    q_b = q.astype(BF16)
    k_t = k.T.astype(BF16)
    lg = -jnp.exp(dec_ref[...])
    t = lax.broadcasted_iota(jnp.int32, (L, 1), 0).astype(F32)
    if reverse:
        e_in = jnp.exp(lg * (L - t))
        w_st = jnp.exp(lg * t)
    else:
        e_in = jnp.exp(lg * (t + 1.0))
        w_st = jnp.exp(lg * (L - 1.0 - t))
    hs = h_ref[...]
    y = jnp.dot(q_b, hs.astype(BF16), preferred_element_type=F32) * e_in
    upd = jnp.dot(k_t, (v * w_st).astype(BF16), preferred_element_type=F32)
    row_head = (lax.broadcasted_iota(jnp.int32, (qk, RET_INNER), 0) % LANES) // (RET_QK_DIM // 2)
    lane_head = lax.broadcasted_iota(jnp.int32, (qk, RET_INNER), 1) // RET_V_DIM
    h_ref[...] = jnp.exp(lg * float(L)) * hs + jnp.where(row_head == lane_head, upd, 0.0)

    if not reverse:
        lgb = -jnp.exp(decb_ref[...])
        q_head = (lax.broadcasted_iota(jnp.int32, (L, qk), 1) % LANES) // (RET_QK_DIM // 2)
        qs = jnp.concatenate([jnp.where(q_head == h, q, 0.0) for h in range(RET_HEADS)], axis=0).astype(BF16)
        sc = jnp.dot(qs, k_t, preferred_element_type=F32)
        dl = (lax.broadcasted_iota(jnp.int32, (L, L), 0) - lax.broadcasted_iota(jnp.int32, (L, L), 1)).astype(F32)
        parts = []
        for h in range(RET_HEADS):
            vs = slice(h * RET_V_DIM, (h + 1) * RET_V_DIM)
            dcomb = jnp.where(dl > 0, jnp.exp(lg[:, vs] * jnp.maximum(dl, 0.0)),
                              jnp.where(dl < 0, jnp.exp(lgb[:, vs] * jnp.maximum(-dl, 0.0)), 2.0))
            p = (sc[h * L:(h + 1) * L] * dcomb).astype(BF16)
            parts.append(jnp.dot(p, v[:, vs].astype(BF16), preferred_element_type=F32))
        o_ref[0] = y + jnp.concatenate(parts, axis=1)
        return
    y = yf_ref[0] + y
    outs = []
    for h in range(RET_HEADS):
        vs = slice(h * RET_V_DIM, (h + 1) * RET_V_DIM)
        yy = y[:, vs]
        mu = jnp.mean(yy, axis=-1, keepdims=True)
        var = jnp.mean(jnp.square(yy - mu), axis=-1, keepdims=True)
        outs.append((yy - mu) * lax.rsqrt(var + EPS) * gng_ref[:, vs] + gnb_ref[:, vs])
    o_ref[0] = _silu(g_ref[0]) * jnp.concatenate(outs, axis=1)


def _ret(u, cos4, sin4, ret_decay, yf, gn_g, gn_b, n_ctx, reverse):
    b, nt, _ = u.shape
    nc = nt // CHUNK
    ncc = n_ctx // CHUNK
    cidx = functools.partial(_chunk_index, reverse=reverse, n_chunks=nc, n_ctx_chunks=ncc)
    qk = RET_HEADS * RET_QK_DIM
    lane_dec = lambda d: jnp.repeat(d, RET_V_DIM).reshape(1, -1)
    in_specs = [pl.BlockSpec((1, CHUNK, qk), lambda bi, s: (bi, cidx(s), COL_RQ // qk)),
                pl.BlockSpec((1, CHUNK, qk), lambda bi, s: (bi, cidx(s), COL_RK // qk)),
                pl.BlockSpec((1, CHUNK, RET_INNER), lambda bi, s: (bi, cidx(s), COL_RV // RET_INNER)),
                pl.BlockSpec((CHUNK, LANES), lambda bi, s: (cidx(s), 0)),
                pl.BlockSpec((CHUNK, LANES), lambda bi, s: (cidx(s), 0)),
                pl.BlockSpec((1, RET_INNER), lambda bi, s: (0, 0))]
    args = [u, u, u, cos4, sin4, lane_dec(ret_decay[1] if reverse else ret_decay[0])]
    if reverse:
        in_specs += [pl.BlockSpec((1, CHUNK, RET_INNER), lambda bi, s: (bi, cidx(s), 0)),
                     pl.BlockSpec((1, CHUNK, RET_INNER), lambda bi, s: (bi, cidx(s), COL_RG // RET_INNER)),
                     pl.BlockSpec((1, RET_INNER), lambda bi, s: (0, 0)),
                     pl.BlockSpec((1, RET_INNER), lambda bi, s: (0, 0))]
        args += [yf, u, gn_g.reshape(1, -1), gn_b.reshape(1, -1)]
    else:
        in_specs += [pl.BlockSpec((1, RET_INNER), lambda bi, s: (0, 0))]
        args += [lane_dec(ret_decay[1])]
    return pl.pallas_call(
        functools.partial(_ret_kernel, reverse=reverse),
        out_shape=jax.ShapeDtypeStruct((b, nt, RET_INNER), F32),
        grid=(b, nc),
        in_specs=in_specs,
        out_specs=pl.BlockSpec((1, CHUNK, RET_INNER), lambda bi, s: (bi, cidx(s), 0)),
        scratch_shapes=[pltpu.VMEM((qk, RET_INNER), F32)],
        compiler_params=_cparams(("parallel", "arbitrary")),
    )(*args)


def _mla_rope(v, c, s1, s2):
    return v * c + pltpu.roll(v, 16, 1) * s1 + pltpu.roll(v, LANES - 16, 1) * s2


def _rms(v, g):
    return v * lax.rsqrt(jnp.mean(v * v, axis=-1, keepdims=True) + EPS) * g


def _mlaprep_kernel(cq_ref, ckv_ref, kr_ref, qg_ref, kvg_ref, wq_ref, wk_ref, wv_ref, c_ref, s1_ref, s2_ref,
                    q_out, k_out, v_out):
    c, s1, s2 = c_ref[...], s1_ref[...], s2_ref[...]
    scale = (MLA_NOPE + MLA_ROPE) ** -0.5
    q = jnp.dot(_rms(cq_ref[0], qg_ref[...]).astype(BF16), wq_ref[...], preferred_element_type=F32)
    ckv = _rms(ckv_ref[0], kvg_ref[...]).astype(BF16)
    kn = jnp.dot(ckv, wk_ref[...], preferred_element_type=F32)
    v_out[0] = jnp.dot(ckv, wv_ref[...], preferred_element_type=F32).astype(BF16)
    krr = _mla_rope(kr_ref[0], c, s1, s2)
    for h in range(MLA_HEADS):
        hs = slice(h * LANES, (h + 1) * LANES)
        q_out[0, :, hs] = (_mla_rope(q[:, hs], c, s1, s2) * scale).astype(BF16)
        k_out[0, :, hs] = (kn[:, hs] + krr).astype(BF16)


def _mlaprep(u, q_g, kv_g, wq, wk, wv, ctab, s1tab, s2tab):
    b, nt, _ = u.shape
    tm = _tile(nt, 1280, 256)
    hw = MLA_HEADS * LANES
    const = lambda shape: pl.BlockSpec(shape, lambda bi, i: (0, 0))
    return pl.pallas_call(
        _mlaprep_kernel,
        out_shape=(jax.ShapeDtypeStruct((b, nt, hw), BF16), jax.ShapeDtypeStruct((b, nt, hw), BF16),
                   jax.ShapeDtypeStruct((b, nt, MLA_HEADS * MLA_V), BF16)),
        grid=(b, nt // tm),
        in_specs=[pl.BlockSpec((1, tm, MLA_Q_RANK), lambda bi, i: (bi, i, COL_CQ // MLA_Q_RANK)),
                  pl.BlockSpec((1, tm, MLA_KV_RANK), lambda bi, i: (bi, i, COL_CKV // MLA_KV_RANK)),
                  pl.BlockSpec((1, tm, LANES), lambda bi, i: (bi, i, COL_KR // LANES)),
                  const((1, MLA_Q_RANK)), const((1, MLA_KV_RANK)),
                  const((MLA_Q_RANK, hw)), const((MLA_KV_RANK, hw)), const((MLA_KV_RANK, MLA_HEADS * MLA_V)),
                  pl.BlockSpec((tm, LANES), lambda bi, i: (i, 0)),
                  pl.BlockSpec((tm, LANES), lambda bi, i: (i, 0)),
                  pl.BlockSpec((tm, LANES), lambda bi, i: (i, 0))],
        out_specs=(pl.BlockSpec((1, tm, hw), lambda bi, i: (bi, i, 0)),
                   pl.BlockSpec((1, tm, hw), lambda bi, i: (bi, i, 0)),
                   pl.BlockSpec((1, tm, MLA_HEADS * MLA_V), lambda bi, i: (bi, i, 0))),
        compiler_params=_cparams(("parallel", "parallel")),
    )(u, u, u, q_g.reshape(1, -1), kv_g.reshape(1, -1), wq, wk, wv, ctab, s1tab, s2tab)


def _flash_kernel(q_ref, k_ref, v_ref, o_ref, m_ref, l_ref, acc_ref, *, tq, tk, n_ctx, nk):
    i = pl.program_id(2)
    j = pl.program_id(3)

    @pl.when(j == 0)
    def _():
        m_ref[...] = jnp.full_like(m_ref, -jnp.inf)
        l_ref[...] = jnp.zeros_like(l_ref)
        acc_ref[...] = jnp.zeros_like(acc_ref)

    def step(masked):
        v = v_ref[0]
        for hh in range(2):
            hs = slice(hh * LANES, (hh + 1) * LANES)
            s = lax.dot_general(q_ref[0, :, hs], k_ref[0, :, hs], (((1,), (1,)), ((), ())),
                                preferred_element_type=F32)
            if masked:
                rows = i * tq + lax.broadcasted_iota(jnp.int32, (tq, tk), 0)
                keys = j * tk + lax.broadcasted_iota(jnp.int32, (tq, tk), 1)
                s = jnp.where(jnp.logical_and(rows < n_ctx, keys >= n_ctx), -1e30, s)
            m_prev = m_ref[hh]
            m_new = jnp.maximum(m_prev, jnp.max(s, axis=-1, keepdims=True))
            alpha = jnp.exp(m_prev - m_new)
            p = jnp.exp(s - m_new)
            l_ref[hh] = alpha * l_ref[hh] + jnp.sum(p, axis=-1, keepdims=True)
            acc_ref[hh] = alpha * acc_ref[hh] + jnp.dot(p.astype(BF16), v, preferred_element_type=F32)
            m_ref[hh] = m_new

    @pl.when(i * tq < n_ctx)
    def _():
        step(True)

    @pl.when(i * tq >= n_ctx)
    def _():
        step(False)

    @pl.when(j == nk - 1)
    def _():
        lane = lax.broadcasted_iota(jnp.int32, (tq, LANES), 1)
        o_ref[0] = jnp.where(lane < MLA_V, acc_ref[0] / l_ref[0], acc_ref[1] / l_ref[1])


def _flash(q, k, v, n_ctx):
    b, nt, _ = q.shape
    tq = _tile(nt, 1280, 256)
    tk = _tile(nt, 1280, 256)
    nk = nt // tk
    pairs = MLA_HEADS // 2
    return pl.pallas_call(
        functools.partial(_flash_kernel, tq=tq, tk=tk, n_ctx=n_ctx, nk=nk),
        out_shape=jax.ShapeDtypeStruct((b, nt, MLA_HEADS * MLA_V), F32),
        grid=(b, pairs, nt // tq, nk),
        in_specs=[pl.BlockSpec((1, tq, 2 * LANES), lambda bi, p, i, j: (bi, i, p)),
                  pl.BlockSpec((1, tk, 2 * LANES), lambda bi, p, i, j: (bi, j, p)),
                  pl.BlockSpec((1, tk, 2 * MLA_V), lambda bi, p, i, j: (bi, j, p))],
        out_specs=pl.BlockSpec((1, tq, 2 * MLA_V), lambda bi, p, i, j: (bi, i, p)),
        scratch_shapes=[pltpu.VMEM((2, tq, 1), F32), pltpu.VMEM((2, tq, 1), F32), pltpu.VMEM((2, tq, LANES), F32)],
        compiler_params=_cparams(("parallel", "parallel", "parallel", "arbitrary")),
    )(q, k, v)


def _merge_kernel(xs_ref, gl_ref, b0_ref, b1_ref, b2_ref, b3_ref, wb_ref, wo_ref, mod_ref, o_ref, *, tm, n_ctx):
    i = pl.program_id(1)
    merged = None
    for n, br in enumerate((b0_ref, b1_ref, b2_ref, b3_ref)):
        proj = jnp.dot(br[0].astype(BF16), wb_ref[n], preferred_element_type=F32)
        term = jax.nn.sigmoid(gl_ref[0, :, n * D_MODEL:(n + 1) * D_MODEL]) * proj
        merged = term if merged is None else merged + term
    out = jnp.dot(merged.astype(BF16), wo_ref[...], preferred_element_type=F32)
    o_ref[0] = xs_ref[0] + _res_gate(mod_ref[0], i * tm, tm, n_ctx, 0) * out


def _merge(xs, u, branches, wb, wo, modtab, n_ctx):
    b, nt, d = xs.shape
    tm = 256
    row = lambda w: pl.BlockSpec((1, tm, w), lambda bi, i: (bi, i, 0))
    return pl.pallas_call(
        functools.partial(_merge_kernel, tm=tm, n_ctx=n_ctx),
        out_shape=jax.ShapeDtypeStruct((b, nt, d), F32),
        grid=(b, nt // tm),
        in_specs=[row(d), row(N_BRANCH * d), row(BRANCH_DIM), row(BRANCH_DIM), row(BRANCH_DIM), row(BRANCH_DIM),
                  pl.BlockSpec((N_BRANCH, BRANCH_DIM, d), lambda bi, i: (0, 0, 0)),
                  pl.BlockSpec((d, d), lambda bi, i: (0, 0)),
                  pl.BlockSpec((1, 16, d), lambda bi, i: (bi, 0, 0))],
        out_specs=row(d),
        compiler_params=_cparams(("parallel", "parallel")),
    )(xs, u, *branches, wb, wo, modtab)


def _ffn_kernel(xs_ref, mod_ref, g_ref, wa_ref, wg_ref, wo_ref, o_ref, h_ref, acc_ref, *, tm, n_ctx, nj):
    i = pl.program_id(1)
    j = pl.program_id(2)

    @pl.when(j == 0)
    def _():
        h_ref[...] = _norm_mod(xs_ref[0], mod_ref[0], g_ref[...], i * tm, n_ctx, 1).astype(BF16)

    h = h_ref[...]
    a = jnp.dot(h, wa_ref[...], preferred_element_type=F32)
    gate = jnp.dot(h, wg_ref[...], preferred_element_type=F32)
    part = jnp.dot((_silu(gate) * a).astype(BF16), wo_ref[...], preferred_element_type=F32)

    @pl.when(j == 0)
    def _():
        acc_ref[...] = part

    @pl.when(j > 0)
    def _():
        acc_ref[...] += part

    @pl.when(j == nj - 1)
    def _():
        o_ref[0] = xs_ref[0] + _res_gate(mod_ref[0], i * tm, tm, n_ctx, 1) * acc_ref[...]


def _ffn(xs, modtab, g, w_in, w_out, n_ctx):
    b, nt, d = xs.shape
    f = w_out.shape[0]
    tm = _tile(nt, 640, 128)
    tf = _tile(f, 1408, LANES)
    nj = f // tf
    return pl.pallas_call(
        functools.partial(_ffn_kernel, tm=tm, n_ctx=n_ctx, nj=nj),
        out_shape=jax.ShapeDtypeStruct((b, nt, d), F32),
        grid=(b, nt // tm, nj),
        in_specs=[pl.BlockSpec((1, tm, d), lambda bi, i, j: (bi, i, 0)),
                  pl.BlockSpec((1, 16, d), lambda bi, i, j: (bi, 0, 0)),
                  pl.BlockSpec((1, d), lambda bi, i, j: (0, 0)),
                  pl.BlockSpec((d, tf), lambda bi, i, j: (0, j)),
                  pl.BlockSpec((d, tf), lambda bi, i, j: (0, nj + j)),
                  pl.BlockSpec((tf, d), lambda bi, i, j: (j, 0))],
        out_specs=pl.BlockSpec((1, tm, d), lambda bi, i, j: (bi, i, 0)),
        scratch_shapes=[pltpu.VMEM((tm, d), BF16), pltpu.VMEM((tm, d), F32)],
        compiler_params=_cparams(("parallel", "parallel", "arbitrary")),
    )(xs, modtab, g.reshape(1, d), w_in, w_in, w_out)


def _final_kernel(x_ref, g_ref, o_ref):
    o_ref[0] = _rms(x_ref[0], g_ref[...])


def _final_norm(xs, g, n_ctx):
    b, nt, d = xs.shape
    t = 256
    skip = n_ctx // t
    return pl.pallas_call(
        _final_kernel,
        out_shape=jax.ShapeDtypeStruct((b, nt - n_ctx, d), F32),
        grid=(b, (nt - n_ctx) // t),
        in_specs=[pl.BlockSpec((1, t, d), lambda bi, i: (bi, i + skip, 0)),
                  pl.BlockSpec((1, d), lambda bi, i: (0, 0))],
        out_specs=pl.BlockSpec((1, t, d), lambda bi, i: (bi, i, 0)),
        compiler_params=_cparams(("parallel", "parallel")),
    )(xs, g.reshape(1, d))


def _prep_w_in(w):
    d = w.shape[0]
    o_conv, o_z, o_xbc, o_dt = 0, 1024, 1536, 2560
    o_rq, o_rk, o_rv, o_rg = 2576, 2832, 3088, 3600
    o_cq, o_ckv, o_kr, o_gl = 4112, 4496, 4752, 4784
    seg = lambda o, n: w[:, o:o + n]

    def halves(o):
        s = seg(o, RET_HEADS * RET_QK_DIM).reshape(d, RET_HEADS, 2, RET_QK_DIM // 2)
        return jnp.transpose(s, (0, 2, 1, 3)).reshape(d, RET_HEADS * RET_QK_DIM)

    zeros = lambda n: jnp.zeros((d, n), w.dtype)
    parts = [seg(o_gl, N_BRANCH * D_MODEL), seg(o_conv, 2 * CONV_DIM), seg(o_xbc, SSM_XBC), seg(o_z, SSM_INNER),
             seg(o_rv, RET_INNER), seg(o_rg, RET_INNER), halves(o_rq), halves(o_rk),
             seg(o_ckv, MLA_KV_RANK), seg(o_cq, MLA_Q_RANK),
             seg(o_dt, 2 * SSM_HEADS), zeros(LANES - 2 * SSM_HEADS),
             zeros(MLA_NOPE), seg(o_kr, MLA_ROPE), zeros(LANES - MLA_NOPE - MLA_ROPE),
             zeros(LANES)]
    out = jnp.concatenate(parts, axis=1).astype(BF16)
    assert out.shape[1] == N_IN_PAD
    return out


def _prep_mla_w(w_uq, w_ukv):
    rq, rkv = w_uq.shape[0], w_ukv.shape[0]
    hd = MLA_NOPE + MLA_ROPE
    wq = jnp.pad(w_uq.reshape(rq, MLA_HEADS, hd), ((0, 0), (0, 0), (0, LANES - hd))).reshape(rq, MLA_HEADS * LANES)
    kv = w_ukv.reshape(rkv, MLA_HEADS, MLA_NOPE + MLA_V)
    wk = jnp.pad(kv[:, :, :MLA_NOPE], ((0, 0), (0, 0), (0, LANES - MLA_NOPE))).reshape(rkv, MLA_HEADS * LANES)
    wv = kv[:, :, MLA_NOPE:].reshape(rkv, MLA_HEADS * MLA_V)
    return wq.astype(BF16), wk.astype(BF16), wv.astype(BF16)


def _axial_angles(n, rot_dim):
    rows = n // GRID_W
    row = jnp.repeat(jnp.arange(rows, dtype=F32), GRID_W)
    col = jnp.tile(jnp.arange(GRID_W, dtype=F32), rows)
    nf = rot_dim // 4
    inv = ROPE_BASE ** (-jnp.arange(nf, dtype=F32) / nf)
    return jnp.concatenate([row[:, None] * inv, col[:, None] * inv], axis=-1)


def _rope_tables(n_lat, n_ctx):
    ang = _axial_angles(n_lat, RET_QK_DIM)
    cos4 = jnp.tile(jnp.cos(ang), (1, RET_HEADS))
    sin4 = jnp.tile(jnp.sin(ang), (1, RET_HEADS))
    cos4 = jnp.concatenate([jnp.ones((n_ctx, LANES), F32), cos4], axis=0)
    sin4 = jnp.concatenate([jnp.zeros((n_ctx, LANES), F32), sin4], axis=0)
    ang = _axial_angles(n_lat, MLA_ROPE)
    c, s = jnp.cos(ang), jnp.sin(ang)
    half = MLA_ROPE // 2
    one = jnp.ones((n_lat, MLA_NOPE), F32)
    zero = lambda w: jnp.zeros((n_lat, w), F32)
    tail = LANES - MLA_NOPE - MLA_ROPE
    ctab = jnp.concatenate([one, c, c, jnp.ones((n_lat, tail), F32)], axis=1)
    s1 = jnp.concatenate([zero(MLA_NOPE + half), s, zero(tail)], axis=1)
    s2 = jnp.concatenate([zero(MLA_NOPE), -s, zero(half + tail)], axis=1)
    ctab = jnp.concatenate([jnp.ones((n_ctx, LANES), F32), ctab], axis=0)
    s1 = jnp.concatenate([jnp.zeros((n_ctx, LANES), F32), s1], axis=0)
    s2 = jnp.concatenate([jnp.zeros((n_ctx, LANES), F32), s2], axis=0)
    return cos4, sin4, ctab, s1, s2


def kernel(x, c, ctx, c_ctx, w_ada, b_ada, norm1_g, norm2_g, w_in, conv_w, conv_b, conv_ln_g, conv_ln_b,
           ssm_conv_w, ssm_conv_b, ssm_dt_bias, ssm_a_log, ssm_d, ssm_norm_g, ret_decay, ret_gn_g, ret_gn_b,
           mla_q_norm_g, mla_kv_norm_g, mla_w_uq, mla_w_ukv, w_branch, w_out, w_ffn_in, w_ffn_out, final_norm_g):
    batch, n_lat, d = x.shape
    n_ctx = ctx.shape[1]
    depth = w_in.shape[0]
    assert d == D_MODEL and n_ctx % 256 == 0 and n_lat % 256 == 0 and batch + 1 <= 8
    cos4, sin4, ctab, s1tab, s2tab = _rope_tables(n_lat, n_ctx)
    xs = jnp.concatenate([ctx, x], axis=1)
    cvec = jnp.zeros((8, d), F32).at[:batch].set(c).at[batch].set(c_ctx)
    for i in range(depth):
        mod = _ada(cvec, w_ada[i], b_ada[i]).reshape(8, 6, d)
        modtab = jnp.zeros((batch, 16, d), F32)
        modtab = modtab.at[:, 0:6].set(jnp.broadcast_to(mod[batch], (batch, 6, d))).at[:, 8:14].set(mod[:batch])
        u = _inproj(xs, modtab, norm1_g[i], _prep_w_in(w_in[i]), n_ctx)
        conv_y = _convmod(u, conv_w[i], conv_b[i], conv_ln_g[i], conv_ln_b[i], n_ctx)
        xa = _ssmconv(u, ssm_conv_w[i], ssm_conv_b[i], n_ctx)
        ssm_f = _ssd(xa, u, ssm_dt_bias[i], ssm_a_log[i], None, None, None, n_ctx, False)
        ssm_y = _ssd(xa, u, ssm_dt_bias[i], ssm_a_log[i], ssm_f, ssm_d[i], ssm_norm_g[i], n_ctx, True)
        ret_f = _ret(u, cos4, sin4, ret_decay[i], None, None, None, n_ctx, False)
        ret_y = _ret(u, cos4, sin4, ret_decay[i], ret_f, ret_gn_g[i], ret_gn_b[i], n_ctx, True)
        wq, wk, wv = _prep_mla_w(mla_w_uq[i], mla_w_ukv[i])
        q, k, v = _mlaprep(u, mla_q_norm_g[i], mla_kv_norm_g[i], wq, wk, wv, ctab, s1tab, s2tab)
        att = _flash(q, k, v, n_ctx)
        xs = _merge(xs, u, (conv_y, ssm_y, ret_y, att), w_branch[i].astype(BF16), w_out[i].astype(BF16),
                    modtab, n_ctx)
        xs = _ffn(xs, modtab, norm2_g[i], w_ffn_in[i].astype(BF16), w_ffn_out[i].astype(BF16), n_ctx)
    return _final_norm(xs, final_norm_g, n_ctx)
```

```python
import functools
import math

import jax
import jax.numpy as jnp
from jax import lax
from jax.experimental import pallas as pl
from jax.experimental.pallas import tpu as pltpu

F32 = jnp.float32
BF16 = jnp.bfloat16
HIGHEST = lax.Precision.HIGHEST

D_MODEL = 1024
GRID_W = 64
CHUNK = 128
ROPE_BASE = 10000.0
EPS = 1e-6
BRANCH_DIM = D_MODEL // 2
N_BRANCH = 4
CONV_DIM = BRANCH_DIM
CONV_WIDTH = 31
SSM_INNER = BRANCH_DIM
SSM_HEAD_DIM = 64
SSM_HEADS = SSM_INNER // SSM_HEAD_DIM
SSM_GROUPS = 2
SSM_STATE = 128
SSM_CONV = 5
SSM_XBC = SSM_INNER + 2 * SSM_GROUPS * SSM_STATE
RET_HEADS = 4
RET_QK_DIM = 64
RET_INNER = BRANCH_DIM
RET_V_DIM = RET_INNER // RET_HEADS
MLA_HEADS = 8
MLA_NOPE = 64
MLA_ROPE = 32
MLA_V = BRANCH_DIM // MLA_HEADS
MLA_Q_RANK = 384
MLA_KV_RANK = 256
FFN_DIM = ((8 * D_MODEL // 3 + 255) // 256) * 256

LANES = 128
VMEM_LIMIT_BYTES = 56 * 1024 * 1024

COL_GL = 0
COL_CONV = 4096
COL_XBC = 5120
COL_Z = 6144
COL_RV = 6656
COL_RG = 7168
COL_RQ = 7680
COL_RK = 7936
COL_CKV = 8192
COL_CQ = 8448
COL_DT = 8832
COL_KR = 8960
N_IN_PAD = 9216


def _tile(n, target, mult):
    best = None
    for t in range(mult, min(n, target) + 1, mult):
        if n % t == 0:
            best = t
    assert best is not None, (n, target, mult)
    return best


def _cparams(sem):
    return pltpu.CompilerParams(dimension_semantics=sem, vmem_limit_bytes=VMEM_LIMIT_BYTES)


def _silu(v):
    return v * jax.nn.sigmoid(v)


def _norm_mod(x, mod, g, row0, n_ctx, k):
    y = x * lax.rsqrt(jnp.mean(x * x, axis=-1, keepdims=True) + EPS) * g
    rows = row0 + lax.broadcasted_iota(jnp.int32, (x.shape[0], 1), 0)
    is_ctx = rows < n_ctx
    shift = jnp.where(is_ctx, mod[3 * k:3 * k + 1], mod[8 + 3 * k:9 + 3 * k])
    scale = jnp.where(is_ctx, mod[3 * k + 1:3 * k + 2], mod[9 + 3 * k:10 + 3 * k])
    return y * (1.0 + scale) + shift


def _res_gate(mod, row0, n_rows, n_ctx, k):
    rows = row0 + lax.broadcasted_iota(jnp.int32, (n_rows, 1), 0)
    return jnp.where(rows < n_ctx, mod[3 * k + 2:3 * k + 3], mod[10 + 3 * k:11 + 3 * k])


def _ada_kernel(c_ref, w_ref, b_ref, o_ref):
    o_ref[...] = jnp.dot(_silu(c_ref[...]), w_ref[...], preferred_element_type=F32, precision=HIGHEST) + b_ref[...]


def _ada(cvec, w, b):
    n = w.shape[1]
    tn = _tile(n, 1536, LANES)
    return pl.pallas_call(
        _ada_kernel,
        out_shape=jax.ShapeDtypeStruct((cvec.shape[0], n), F32),
        grid=(n // tn,),
        in_specs=[pl.BlockSpec(cvec.shape, lambda j: (0, 0)),
                  pl.BlockSpec((w.shape[0], tn), lambda j: (0, j)),
                  pl.BlockSpec((1, tn), lambda j: (0, j))],
        out_specs=pl.BlockSpec((cvec.shape[0], tn), lambda j: (0, j)),
        compiler_params=_cparams(("arbitrary",)),
    )(cvec, w, b.reshape(1, n))


def _inproj_kernel(x_ref, mod_ref, g_ref, w_ref, o_ref, h_ref, *, tm, n_ctx):
    i = pl.program_id(1)

    @pl.when(pl.program_id(2) == 0)
    def _():
        h_ref[...] = _norm_mod(x_ref[0], mod_ref[0], g_ref[...], i * tm, n_ctx, 0).astype(BF16)

    o_ref[0] = jnp.dot(h_ref[...], w_ref[...], preferred_element_type=F32)


def _inproj(xs, modtab, g, w, n_ctx):
    b, nt, d = xs.shape
    n = w.shape[1]
    tm = _tile(nt, 1280, 256)
    tn = _tile(n, 1024, LANES)
    return pl.pallas_call(
        functools.partial(_inproj_kernel, tm=tm, n_ctx=n_ctx),
        out_shape=jax.ShapeDtypeStruct((b, nt, n), F32),
        grid=(b, nt // tm, n // tn),
        in_specs=[pl.BlockSpec((1, tm, d), lambda bi, i, j: (bi, i, 0)),
                  pl.BlockSpec((1, 16, d), lambda bi, i, j: (bi, 0, 0)),
                  pl.BlockSpec((1, d), lambda bi, i, j: (0, 0)),
                  pl.BlockSpec((d, tn), lambda bi, i, j: (0, j))],
        out_specs=pl.BlockSpec((1, tm, tn), lambda bi, i, j: (bi, i, j)),
        scratch_shapes=[pltpu.VMEM((tm, d), BF16)],
        compiler_params=_cparams(("parallel", "parallel", "arbitrary")),
    )(xs, modtab, g.reshape(1, d), w)


def _halo_flags(i, t, n_ctx, n_tiles):
    zero_prev = jnp.logical_or(i == 0, i * t == n_ctx)
    zero_next = jnp.logical_or((i + 1) * t == n_ctx, i == n_tiles - 1)
    return zero_prev, zero_next


def _convmod_kernel(cur_ref, prev_ref, next_ref, w_ref, b_ref, lg_ref, lb_ref, o_ref, buf_ref, acc_ref,
                    *, t, n_ctx, n_tiles):
    i = pl.program_id(1)
    zero_prev, zero_next = _halo_flags(i, t, n_ctx, n_tiles)

    def glu(u):
        return u[:, :CONV_DIM] * jax.nn.sigmoid(u[:, CONV_DIM:])

    halo = 16
    buf_ref[0:halo, :] = jnp.where(zero_prev, 0.0, glu(prev_ref[0]))
    buf_ref[halo:halo + t, :] = glu(cur_ref[0])
    buf_ref[halo + t:2 * halo + t, :] = jnp.where(zero_next, 0.0, glu(next_ref[0]))
    pad = CONV_WIDTH // 2
    rc = 64
    for r in range(t // rc):
        for c in range(CONV_DIM // LANES):
            cs = slice(c * LANES, (c + 1) * LANES)
            acc = jnp.zeros((rc, LANES), F32)
            for k in range(CONV_WIDTH):
                off = halo + r * rc + k - pad
                acc = acc + w_ref[k:k + 1, cs] * buf_ref[off:off + rc, cs]
            acc_ref[r * rc:(r + 1) * rc, cs] = acc + b_ref[:, cs]
    h = acc_ref[...]
    mu = jnp.mean(h, axis=-1, keepdims=True)
    var = jnp.mean(jnp.square(h - mu), axis=-1, keepdims=True)
    y = (h - mu) * lax.rsqrt(var + EPS) * lg_ref[...] + lb_ref[...]
    o_ref[0] = _silu(y)


def _convmod(u, w, bias, ln_g, ln_b, n_ctx):
    b, nt, _ = u.shape
    t = 256
    n_tiles = nt // t
    hb = t // 16
    last_hb = nt // 16 - 1
    width = 2 * CONV_DIM
    cb = COL_CONV // width
    return pl.pallas_call(
        functools.partial(_convmod_kernel, t=t, n_ctx=n_ctx, n_tiles=n_tiles),
        out_shape=jax.ShapeDtypeStruct((b, nt, CONV_DIM), F32),
        grid=(b, n_tiles),
        in_specs=[pl.BlockSpec((1, t, width), lambda bi, i: (bi, i, cb)),
                  pl.BlockSpec((1, 16, width), lambda bi, i: (bi, jnp.maximum(i * hb - 1, 0), cb)),
                  pl.BlockSpec((1, 16, width), lambda bi, i: (bi, jnp.minimum((i + 1) * hb, last_hb), cb)),
                  pl.BlockSpec((CONV_WIDTH, CONV_DIM), lambda bi, i: (0, 0)),
                  pl.BlockSpec((1, CONV_DIM), lambda bi, i: (0, 0)),
                  pl.BlockSpec((1, CONV_DIM), lambda bi, i: (0, 0)),
                  pl.BlockSpec((1, CONV_DIM), lambda bi, i: (0, 0))],
        out_specs=pl.BlockSpec((1, t, CONV_DIM), lambda bi, i: (bi, i, 0)),
        scratch_shapes=[pltpu.VMEM((t + 32, CONV_DIM), F32), pltpu.VMEM((t, CONV_DIM), F32)],
        compiler_params=_cparams(("parallel", "parallel")),
    )(u, u, u, w, bias.reshape(1, -1), ln_g.reshape(1, -1), ln_b.reshape(1, -1))


def _ssmconv_kernel(cur_ref, prev_ref, next_ref, w_ref, b_ref, o_ref, buf_ref, *, t, n_ctx, n_tiles):
    i = pl.program_id(1)
    zero_prev, zero_next = _halo_flags(i, t, n_ctx, n_tiles)
    halo = 8
    buf_ref[0:halo, :] = jnp.where(zero_prev, 0.0, prev_ref[0])
    buf_ref[halo:halo + t, :] = cur_ref[0]
    buf_ref[halo + t:2 * halo + t, :] = jnp.where(zero_next, 0.0, next_ref[0])
    pad = SSM_CONV // 2
    rc = 64
    for r in range(t // rc):
        for c in range(SSM_XBC // LANES):
            cs = slice(c * LANES, (c + 1) * LANES)
            acc = jnp.zeros((rc, LANES), F32)
            for k in range(SSM_CONV):
                off = halo + r * rc + k - pad
                acc = acc + w_ref[k:k + 1, cs] * buf_ref[off:off + rc, cs]
            o_ref[0, r * rc:(r + 1) * rc, cs] = _silu(acc + b_ref[:, cs])


def _ssmconv(u, w, bias, n_ctx):
    b, nt, _ = u.shape
    t = 256
    n_tiles = nt // t
    hb = t // 8
    last_hb = nt // 8 - 1
    cb = COL_XBC // SSM_XBC
    return pl.pallas_call(
        functools.partial(_ssmconv_kernel, t=t, n_ctx=n_ctx, n_tiles=n_tiles),
        out_shape=jax.ShapeDtypeStruct((b, nt, SSM_XBC), F32),
        grid=(b, n_tiles),
        in_specs=[pl.BlockSpec((1, t, SSM_XBC), lambda bi, i: (bi, i, cb)),
                  pl.BlockSpec((1, 8, SSM_XBC), lambda bi, i: (bi, jnp.maximum(i * hb - 1, 0), cb)),
                  pl.BlockSpec((1, 8, SSM_XBC), lambda bi, i: (bi, jnp.minimum((i + 1) * hb, last_hb), cb)),
                  pl.BlockSpec((SSM_CONV, SSM_XBC), lambda bi, i: (0, 0)),
                  pl.BlockSpec((1, SSM_XBC), lambda bi, i: (0, 0))],
        out_specs=pl.BlockSpec((1, t, SSM_XBC), lambda bi, i: (bi, i, 0)),
        scratch_shapes=[pltpu.VMEM((t + 16, SSM_XBC), F32)],
        compiler_params=_cparams(("parallel", "parallel")),
    )(u, u, u, w, bias.reshape(1, -1))


def _chunk_index(s, reverse, n_chunks, n_ctx_chunks):
    if not reverse:
        return s
    return jnp.where(s < n_ctx_chunks, n_ctx_chunks - 1 - s, n_chunks + n_ctx_chunks - 1 - s)


def _softplus(v):
    return jnp.maximum(v, 0.0) + jnp.log(1.0 + jnp.exp(-jnp.abs(v)))


def _ssd_kernel(*refs, reverse):
    if reverse:
        xa_ref, dt_ref, dtb_ref, alog_ref, yf_ref, z_ref, dskip_ref, ng_ref, o_ref, h_ref = refs
    else:
        xa_ref, dt_ref, dtb_ref, alog_ref, o_ref, h_ref = refs

    @pl.when(pl.program_id(1) == 0)
    def _():
        h_ref[...] = jnp.zeros_like(h_ref)

    L = CHUNK
    xa = xa_ref[0]
    dt = _softplus(dt_ref[0] + dtb_ref[...])
    la = dt * (-jnp.exp(alog_ref[...]))
    r = lax.broadcasted_iota(jnp.int32, (L, L), 0)
    c = lax.broadcasted_iota(jnp.int32, (L, L), 1)
    keep = (r <= c) if reverse else (r >= c)
    cum = jnp.dot(keep.astype(F32), la, preferred_element_type=F32, precision=HIGHEST)
    cum_t = cum.T
    off = SSM_HEADS if reverse else 0
    tot_row = 0 if reverse else L - 1
    hpg = SSM_HEADS // SSM_GROUPS
    gw = hpg * SSM_HEAD_DIM
    lane_head = lax.broadcasted_iota(jnp.int32, (L, gw), 1) // SSM_HEAD_DIM
    ys = []
    for g in range(SSM_GROUPS):
        b_f = xa[:, SSM_INNER + g * SSM_STATE:SSM_INNER + (g + 1) * SSM_STATE]
        c_b = xa[:, SSM_INNER + (SSM_GROUPS + g) * SSM_STATE:SSM_INNER + (SSM_GROUPS + g + 1) * SSM_STATE].astype(BF16)
        gram = lax.dot_general(c_b, b_f.astype(BF16), (((1,), (1,)), ((), ())), preferred_element_type=F32)
        b_t = b_f.T.astype(BF16)

        def expand(mat):
            out = jnp.broadcast_to(mat[:, off + g * hpg + hpg - 1:off + g * hpg + hpg], (L, gw))
            for j in range(hpg - 2, -1, -1):
                out = jnp.where(lane_head == j, mat[:, off + g * hpg + j:off + g * hpg + j + 1], out)
            return out

        dt_l = expand(dt)
        cum_l = expand(cum)
        tot_l = cum_l[tot_row:tot_row + 1, :]
        xdt = xa[:, g * gw:(g + 1) * gw] * dt_l
        xdt_b = xdt.astype(BF16)
        ms = []
        for j in range(hpg):
            hh = off + g * hpg + j
            seg = cum[:, hh:hh + 1] - cum_t[hh:hh + 1, :]
            dec = jnp.exp(jnp.where(keep, seg, -jnp.inf))
            ms.append((gram * dec).astype(BF16))
        rr = jnp.dot(jnp.concatenate(ms, axis=0), xdt_b, preferred_element_type=F32)
        y = jnp.where(lane_head == 0, rr[0:L], 0.0)
        for j in range(1, hpg):
            y = y + jnp.where(lane_head == j, rr[j * L:(j + 1) * L], 0.0)
        hg = h_ref[g]
        y = y + jnp.dot(c_b, hg.astype(BF16), preferred_element_type=F32) * jnp.exp(cum_l)
        wm = jnp.exp(tot_l - cum_l)
        h_ref[g] = jnp.exp(tot_l) * hg + jnp.dot(b_t, (xdt * wm).astype(BF16), preferred_element_type=F32)
        ys.append(y)
    y = jnp.concatenate(ys, axis=1)
    if not reverse:
        o_ref[0] = y
        return
    y = yf_ref[0] + y + dskip_ref[...] * xa[:, :SSM_INNER]
    y = y * _silu(z_ref[0])
    outs = []
    for g in range(SSM_GROUPS):
        yy = y[:, g * gw:(g + 1) * gw]
        outs.append(yy * lax.rsqrt(jnp.mean(yy * yy, axis=-1, keepdims=True) + EPS) * ng_ref[:, g * gw:(g + 1) * gw])
    o_ref[0] = jnp.concatenate(outs, axis=1)


def _ssd(xa, u, dt_bias, a_log, yf, d_skip, norm_g, n_ctx, reverse):
    b, nt, _ = xa.shape
    nc = nt // CHUNK
    ncc = n_ctx // CHUNK
    cidx = functools.partial(_chunk_index, reverse=reverse, n_chunks=nc, n_ctx_chunks=ncc)
    pad16 = lambda p: jnp.pad(p.reshape(1, -1), ((0, 0), (0, LANES - 2 * SSM_HEADS)))
    in_specs = [pl.BlockSpec((1, CHUNK, SSM_XBC), lambda bi, s: (bi, cidx(s), 0)),
                pl.BlockSpec((1, CHUNK, LANES), lambda bi, s: (bi, cidx(s), COL_DT // LANES)),
                pl.BlockSpec((1, LANES), lambda bi, s: (0, 0)),
                pl.BlockSpec((1, LANES), lambda bi, s: (0, 0))]
    args = [xa, u, pad16(dt_bias), pad16(a_log)]
    if reverse:
        in_specs += [pl.BlockSpec((1, CHUNK, SSM_INNER), lambda bi, s: (bi, cidx(s), 0)),
                     pl.BlockSpec((1, CHUNK, SSM_INNER), lambda bi, s: (bi, cidx(s), COL_Z // SSM_INNER)),
                     pl.BlockSpec((1, SSM_INNER), lambda bi, s: (0, 0)),
                     pl.BlockSpec((1, SSM_INNER), lambda bi, s: (0, 0))]
        args += [yf, u, jnp.repeat(d_skip, SSM_HEAD_DIM).reshape(1, -1), norm_g.reshape(1, -1)]
    return pl.pallas_call(
        functools.partial(_ssd_kernel, reverse=reverse),
        out_shape=jax.ShapeDtypeStruct((b, nt, SSM_INNER), F32),
        grid=(b, nc),
        in_specs=in_specs,
        out_specs=pl.BlockSpec((1, CHUNK, SSM_INNER), lambda bi, s: (bi, cidx(s), 0)),
        scratch_shapes=[pltpu.VMEM((SSM_GROUPS, SSM_STATE, SSM_INNER // SSM_GROUPS), F32)],
        compiler_params=_cparams(("parallel", "arbitrary")),
    )(*args)


def _ret_rope(v, cos, sin):
    lo, hi = v[:, :LANES], v[:, LANES:]
    return jnp.concatenate([lo * cos - hi * sin, lo * sin + hi * cos], axis=1)


def _ret_kernel(*refs, reverse):
    if reverse:
        (q_ref, k_ref, v_ref, cos_ref, sin_ref, dec_ref, yf_ref, g_ref, gng_ref, gnb_ref, o_ref, h_ref) = refs
    else:
        (q_ref, k_ref, v_ref, cos_ref, sin_ref, dec_ref, decb_ref, o_ref, h_ref) = refs

    @pl.when(pl.program_id(1) == 0)
    def _():
        h_ref[...] = jnp.zeros_like(h_ref)

    L = CHUNK
    qk = RET_HEADS * RET_QK_DIM
    cos, sin = cos_ref[...], sin_ref[...]
    q = _ret_rope(q_ref[0], cos, sin)
    k = _ret_rope(k_ref[0], cos, sin) * (RET_QK_DIM ** -0.5)
    v = v_ref[0]
    q_b = q.astype(BF16)
    k_t = k.T.astype(BF16)
    lg = -jnp.exp(dec_ref[...])
    t = lax.broadcasted_iota(jnp.int32, (L, 1), 0).astype(F32)
    if reverse:
        e_in = jnp.exp(lg * (L - t))
        w_st = jnp.exp(lg * t)
    else:
        e_in = jnp.exp(lg * (t + 1.0))
        w_st = jnp.exp(lg * (L - 1.0 - t))
    hs = h_ref[...]
    y = jnp.dot(q_b, hs.astype(BF16), preferred_element_type=F32) * e_in
    upd = jnp.dot(k_t, (v * w_st).astype(BF16), preferred_element_type=F32)
    row_head = (lax.broadcasted_iota(jnp.int32, (qk, RET_INNER), 0) % LANES) // (RET_QK_DIM // 2)
    lane_head = lax.broadcasted_iota(jnp.int32, (qk, RET_INNER), 1) // RET_V_DIM
    h_ref[...] = jnp.exp(lg * float(L)) * hs + jnp.where(row_head == lane_head, upd, 0.0)

    if not reverse:
        lgb = -jnp.exp(decb_ref[...])
        q_head = (lax.broadcasted_iota(jnp.int32, (L, qk), 1) % LANES) // (RET_QK_DIM // 2)
        qs = jnp.concatenate([jnp.where(q_head == h, q, 0.0) for h in range(RET_HEADS)], axis=0).astype(BF16)
        sc = jnp.dot(qs, k_t, preferred_element_type=F32)
        dl = (lax.broadcasted_iota(jnp.int32, (L, L), 0) - lax.broadcasted_iota(jnp.int32, (L, L), 1)).astype(F32)
        parts = []
        for h in range(RET_HEADS):
            vs = slice(h * RET_V_DIM, (h + 1) * RET_V_DIM)
            dcomb = jnp.where(dl > 0, jnp.exp(lg[:, vs] * jnp.maximum(dl, 0.0)),
                              jnp.where(dl < 0, jnp.exp(lgb[:, vs] * jnp.maximum(-dl, 0.0)), 2.0))
            p = (sc[h * L:(h + 1) * L] * dcomb).astype(BF16)
            parts.append(jnp.dot(p, v[:, vs].astype(BF16), preferred_element_type=F32))
        o_ref[0] = y + jnp.concatenate(parts, axis=1)
        return
    y = yf_ref[0] + y
    outs = []
    for h in range(RET_HEADS):
        vs = slice(h * RET_V_DIM, (h + 1) * RET_V_DIM)
        yy = y[:, vs]
        mu = jnp.mean(yy, axis=-1, keepdims=True)
        var = jnp.mean(jnp.square(yy - mu), axis=-1, keepdims=True)
        outs.append((yy - mu) * lax.rsqrt(var + EPS) * gng_ref[:, vs] + gnb_ref[:, vs])
    o_ref[0] = _silu(g_ref[0]) * jnp.concatenate(outs, axis=1)


def _ret(u, cos4, sin4, ret_decay, yf, gn_g, gn_b, n_ctx, reverse):
    b, nt, _ = u.shape
    nc = nt // CHUNK
    ncc = n_ctx // CHUNK
    cidx = functools.partial(_chunk_index, reverse=reverse, n_chunks=nc, n_ctx_chunks=ncc)
    qk = RET_HEADS * RET_QK_DIM
    lane_dec = lambda d: jnp.repeat(d, RET_V_DIM).reshape(1, -1)
    in_specs = [pl.BlockSpec((1, CHUNK, qk), lambda bi, s: (bi, cidx(s), COL_RQ // qk)),
                pl.BlockSpec((1, CHUNK, qk), lambda bi, s: (bi, cidx(s), COL_RK // qk)),
                pl.BlockSpec((1, CHUNK, RET_INNER), lambda bi, s: (bi, cidx(s), COL_RV // RET_INNER)),
                pl.BlockSpec((CHUNK, LANES), lambda bi, s: (cidx(s), 0)),
                pl.BlockSpec((CHUNK, LANES), lambda bi, s: (cidx(s), 0)),
                pl.BlockSpec((1, RET_INNER), lambda bi, s: (0, 0))]
    args = [u, u, u, cos4, sin4, lane_dec(ret_decay[1] if reverse else ret_decay[0])]
    if reverse:
        in_specs += [pl.BlockSpec((1, CHUNK, RET_INNER), lambda bi, s: (bi, cidx(s), 0)),
                     pl.BlockSpec((1, CHUNK, RET_INNER), lambda bi, s: (bi, cidx(s), COL_RG // RET_INNER)),
                     pl.BlockSpec((1, RET_INNER), lambda bi, s: (0, 0)),
                     pl.BlockSpec((1, RET_INNER), lambda bi, s: (0, 0))]
        args += [yf, u, gn_g.reshape(1, -1), gn_b.reshape(1, -1)]
    else:
        in_specs += [pl.BlockSpec((1, RET_INNER), lambda bi, s: (0, 0))]
        args += [lane_dec(ret_decay[1])]
    return pl.pallas_call(
        functools.partial(_ret_kernel, reverse=reverse),
        out_shape=jax.ShapeDtypeStruct((b, nt, RET_INNER), F32),
        grid=(b, nc),
        in_specs=in_specs,
        out_specs=pl.BlockSpec((1, CHUNK, RET_INNER), lambda bi, s: (bi, cidx(s), 0)),
        scratch_shapes=[pltpu.VMEM((qk, RET_INNER), F32)],
        compiler_params=_cparams(("parallel", "arbitrary")),
    )(*args)


def _mla_rope(v, c, s1, s2):
    return v * c + pltpu.roll(v, 16, 1) * s1 + pltpu.roll(v, LANES - 16, 1) * s2


def _rms(v, g):
    return v * lax.rsqrt(jnp.mean(v * v, axis=-1, keepdims=True) + EPS) * g


def _mlaprep_kernel(cq_ref, ckv_ref, kr_ref, qg_ref, kvg_ref, wq_ref, wk_ref, wv_ref, c_ref, s1_ref, s2_ref,
                    qt_out, k_out, vt_out):
    c, s1, s2 = c_ref[...], s1_ref[...], s2_ref[...]
    scale = (MLA_NOPE + MLA_ROPE) ** -0.5 * math.log2(math.e)
    q = jnp.dot(_rms(cq_ref[0], qg_ref[...]).astype(BF16), wq_ref[...], preferred_element_type=F32)
    ckv = _rms(ckv_ref[0], kvg_ref[...]).astype(BF16)
    kn = jnp.dot(ckv, wk_ref[...], preferred_element_type=F32)
    vt_out[0] = jnp.dot(ckv, wv_ref[...], preferred_element_type=F32).T.astype(BF16)
    krr = _mla_rope(kr_ref[0], c, s1, s2)
    for h in range(MLA_HEADS):
        hs = slice(h * LANES, (h + 1) * LANES)
        qt_out[0, hs, :] = (_mla_rope(q[:, hs], c, s1, s2) * scale).T.astype(BF16)
        k_out[0, :, hs] = (kn[:, hs] + krr).astype(BF16)


def _mlaprep(u, q_g, kv_g, wq, wk, wv, ctab, s1tab, s2tab):
    b, nt, _ = u.shape
    tm = _tile(nt, 1280, 256)
    hw = MLA_HEADS * LANES
    const = lambda shape: pl.BlockSpec(shape, lambda bi, i: (0, 0))
    return pl.pallas_call(
        _mlaprep_kernel,
        out_shape=(jax.ShapeDtypeStruct((b, hw, nt), BF16), jax.ShapeDtypeStruct((b, nt, hw), BF16),
                   jax.ShapeDtypeStruct((b, MLA_HEADS * MLA_V, nt), BF16)),
        grid=(b, nt // tm),
        in_specs=[pl.BlockSpec((1, tm, MLA_Q_RANK), lambda bi, i: (bi, i, COL_CQ // MLA_Q_RANK)),
                  pl.BlockSpec((1, tm, MLA_KV_RANK), lambda bi, i: (bi, i, COL_CKV // MLA_KV_RANK)),
                  pl.BlockSpec((1, tm, LANES), lambda bi, i: (bi, i, COL_KR // LANES)),
                  const((1, MLA_Q_RANK)), const((1, MLA_KV_RANK)),
                  const((MLA_Q_RANK, hw)), const((MLA_KV_RANK, hw)), const((MLA_KV_RANK, MLA_HEADS * MLA_V)),
                  pl.BlockSpec((tm, LANES), lambda bi, i: (i, 0)),
                  pl.BlockSpec((tm, LANES), lambda bi, i: (i, 0)),
                  pl.BlockSpec((tm, LANES), lambda bi, i: (i, 0))],
        out_specs=(pl.BlockSpec((1, hw, tm), lambda bi, i: (bi, 0, i)),
                   pl.BlockSpec((1, tm, hw), lambda bi, i: (bi, i, 0)),
                   pl.BlockSpec((1, MLA_HEADS * MLA_V, tm), lambda bi, i: (bi, 0, i))),
        compiler_params=_cparams(("parallel", "parallel")),
    )(u, u, u, q_g.reshape(1, -1), kv_g.reshape(1, -1), wq, wk, wv, ctab, s1tab, s2tab)


ACC_ROWS = MLA_V + 16


def _flash_kernel(k_ref, qt_ref, vt_ref, o_ref, m_ref, acc_ref, *, tq, tk, n_ctx, nk):
    i = pl.program_id(2)
    j = pl.program_id(3)

    @pl.when(j == 0)
    def _():
        m_ref[...] = jnp.full_like(m_ref, -jnp.inf)
        acc_ref[...] = jnp.zeros_like(acc_ref)

    def step(masked):
        ones = jnp.ones((ACC_ROWS - MLA_V, tk), BF16)
        for hh in range(2):
            hs = slice(hh * LANES, (hh + 1) * LANES)
            s = jnp.dot(k_ref[0, :, hs], qt_ref[0, hs, :], preferred_element_type=F32)
            if masked:
                keys = j * tk + lax.broadcasted_iota(jnp.int32, (tk, tq), 0)
                cols = i * tq + lax.broadcasted_iota(jnp.int32, (tk, tq), 1)
                s = jnp.where(jnp.logical_and(cols < n_ctx, keys >= n_ctx), -1e30, s)
            m_prev = m_ref[hh]
            m_new = jnp.maximum(m_prev, jnp.max(s, axis=0, keepdims=True))
            alpha = jnp.exp2(m_prev - m_new)
            p = jnp.exp2(s - m_new).astype(BF16)
            vs = slice(hh * MLA_V, (hh + 1) * MLA_V)
            lhs = jnp.concatenate([vt_ref[0, vs, :], ones], axis=0)
            acc_ref[hh] = alpha * acc_ref[hh] + jnp.dot(lhs, p, preferred_element_type=F32)
            m_ref[hh] = m_new

    @pl.when(i * tq < n_ctx)
    def _():
        step(True)

    @pl.when(i * tq >= n_ctx)
    def _():
        step(False)

    @pl.when(j == nk - 1)
    def _():
        outs = []
        for hh in range(2):
            a = acc_ref[hh]
            outs.append(a[:MLA_V] * (1.0 / a[MLA_V:MLA_V + 1]))
        o_ref[0] = jnp.concatenate(outs, axis=0).T


def _flash(k, qt, vt, n_ctx, tq_target=3328, tk_target=640):
    b, nt, _ = k.shape
    tq = _tile(nt, tq_target, LANES)
    tk = _tile(nt, tk_target, LANES)
    nk = nt // tk
    pairs = MLA_HEADS // 2
    return pl.pallas_call(
        functools.partial(_flash_kernel, tq=tq, tk=tk, n_ctx=n_ctx, nk=nk),
        out_shape=jax.ShapeDtypeStruct((b, nt, MLA_HEADS * MLA_V), F32),
        grid=(b, pairs, nt // tq, nk),
        in_specs=[pl.BlockSpec((1, tk, 2 * LANES), lambda bi, p, i, j: (bi, j, p)),
                  pl.BlockSpec((1, 2 * LANES, tq), lambda bi, p, i, j: (bi, p, i)),
                  pl.BlockSpec((1, 2 * MLA_V, tk), lambda bi, p, i, j: (bi, p, j))],
        out_specs=pl.BlockSpec((1, tq, 2 * MLA_V), lambda bi, p, i, j: (bi, i, p)),
        scratch_shapes=[pltpu.VMEM((2, 1, tq), F32), pltpu.VMEM((2, ACC_ROWS, tq), F32)],
        compiler_params=_cparams(("parallel", "parallel", "parallel", "arbitrary")),
    )(k, qt, vt)


def _merge_kernel(xs_ref, gl_ref, b0_ref, b1_ref, b2_ref, b3_ref, wb_ref, wo_ref, mod_ref, o_ref, *, tm, n_ctx):
    i = pl.program_id(1)
    merged = None
    for n, br in enumerate((b0_ref, b1_ref, b2_ref, b3_ref)):
        proj = jnp.dot(br[0].astype(BF16), wb_ref[n], preferred_element_type=F32)
        term = jax.nn.sigmoid(gl_ref[0, :, n * D_MODEL:(n + 1) * D_MODEL]) * proj
        merged = term if merged is None else merged + term
    out = jnp.dot(merged.astype(BF16), wo_ref[...], preferred_element_type=F32)
    o_ref[0] = xs_ref[0] + _res_gate(mod_ref[0], i * tm, tm, n_ctx, 0) * out


def _merge(xs, u, branches, wb, wo, modtab, n_ctx):
    b, nt, d = xs.shape
    tm = 256
    row = lambda w: pl.BlockSpec((1, tm, w), lambda bi, i: (bi, i, 0))
    return pl.pallas_call(
        functools.partial(_merge_kernel, tm=tm, n_ctx=n_ctx),
        out_shape=jax.ShapeDtypeStruct((b, nt, d), F32),
        grid=(b, nt // tm),
        in_specs=[row(d), row(N_BRANCH * d), row(BRANCH_DIM), row(BRANCH_DIM), row(BRANCH_DIM), row(BRANCH_DIM),
                  pl.BlockSpec((N_BRANCH, BRANCH_DIM, d), lambda bi, i: (0, 0, 0)),
                  pl.BlockSpec((d, d), lambda bi, i: (0, 0)),
                  pl.BlockSpec((1, 16, d), lambda bi, i: (bi, 0, 0))],
        out_specs=row(d),
        compiler_params=_cparams(("parallel", "parallel")),
    )(xs, u, *branches, wb, wo, modtab)


def _ffn_kernel(xs_ref, mod_ref, g_ref, wa_ref, wg_ref, wo_ref, o_ref, h_ref, acc_ref, *, tm, n_ctx, nj):
    i = pl.program_id(1)
    j = pl.program_id(2)

    @pl.when(j == 0)
    def _():
        h_ref[...] = _norm_mod(xs_ref[0], mod_ref[0], g_ref[...], i * tm, n_ctx, 1).astype(BF16)

    h = h_ref[...]
    a = jnp.dot(h, wa_ref[...], preferred_element_type=F32)
    gate = jnp.dot(h, wg_ref[...], preferred_element_type=F32)
    part = jnp.dot((_silu(gate) * a).astype(BF16), wo_ref[...], preferred_element_type=F32)

    @pl.when(j == 0)
    def _():
        acc_ref[...] = part

    @pl.when(j > 0)
    def _():
        acc_ref[...] += part

    @pl.when(j == nj - 1)
    def _():
        o_ref[0] = xs_ref[0] + _res_gate(mod_ref[0], i * tm, tm, n_ctx, 1) * acc_ref[...]


def _ffn(xs, modtab, g, w_in, w_out, n_ctx):
    b, nt, d = xs.shape
    f = w_out.shape[0]
    tm = _tile(nt, 640, 128)
    tf = _tile(f, 1408, LANES)
    nj = f // tf
    return pl.pallas_call(
        functools.partial(_ffn_kernel, tm=tm, n_ctx=n_ctx, nj=nj),
        out_shape=jax.ShapeDtypeStruct((b, nt, d), F32),
        grid=(b, nt // tm, nj),
        in_specs=[pl.BlockSpec((1, tm, d), lambda bi, i, j: (bi, i, 0)),
                  pl.BlockSpec((1, 16, d), lambda bi, i, j: (bi, 0, 0)),
                  pl.BlockSpec((1, d), lambda bi, i, j: (0, 0)),
                  pl.BlockSpec((d, tf), lambda bi, i, j: (0, j)),
                  pl.BlockSpec((d, tf), lambda bi, i, j: (0, nj + j)),
                  pl.BlockSpec((tf, d), lambda bi, i, j: (j, 0))],
        out_specs=pl.BlockSpec((1, tm, d), lambda bi, i, j: (bi, i, 0)),
        scratch_shapes=[pltpu.VMEM((tm, d), BF16), pltpu.VMEM((tm, d), F32)],
        compiler_params=_cparams(("parallel", "parallel", "arbitrary")),
    )(xs, modtab, g.reshape(1, d), w_in, w_in, w_out)


def _final_kernel(x_ref, g_ref, o_ref):
    o_ref[0] = _rms(x_ref[0], g_ref[...])


def _final_norm(xs, g, n_ctx):
    b, nt, d = xs.shape
    t = 256
    skip = n_ctx // t
    return pl.pallas_call(
        _final_kernel,
        out_shape=jax.ShapeDtypeStruct((b, nt - n_ctx, d), F32),
        grid=(b, (nt - n_ctx) // t),
        in_specs=[pl.BlockSpec((1, t, d), lambda bi, i: (bi, i + skip, 0)),
                  pl.BlockSpec((1, d), lambda bi, i: (0, 0))],
        out_specs=pl.BlockSpec((1, t, d), lambda bi, i: (bi, i, 0)),
        compiler_params=_cparams(("parallel", "parallel")),
    )(xs, g.reshape(1, d))


def _prep_w_in(w):
    d = w.shape[0]
    o_conv, o_z, o_xbc, o_dt = 0, 1024, 1536, 2560
    o_rq, o_rk, o_rv, o_rg = 2576, 2832, 3088, 3600
    o_cq, o_ckv, o_kr, o_gl = 4112, 4496, 4752, 4784
    seg = lambda o, n: w[:, o:o + n]

    def halves(o):
        s = seg(o, RET_HEADS * RET_QK_DIM).reshape(d, RET_HEADS, 2, RET_QK_DIM // 2)
        return jnp.transpose(s, (0, 2, 1, 3)).reshape(d, RET_HEADS * RET_QK_DIM)

    zeros = lambda n: jnp.zeros((d, n), w.dtype)
    parts = [seg(o_gl, N_BRANCH * D_MODEL), seg(o_conv, 2 * CONV_DIM), seg(o_xbc, SSM_XBC), seg(o_z, SSM_INNER),
             seg(o_rv, RET_INNER), seg(o_rg, RET_INNER), halves(o_rq), halves(o_rk),
             seg(o_ckv, MLA_KV_RANK), seg(o_cq, MLA_Q_RANK),
             seg(o_dt, 2 * SSM_HEADS), zeros(LANES - 2 * SSM_HEADS),
             zeros(MLA_NOPE), seg(o_kr, MLA_ROPE), zeros(LANES - MLA_NOPE - MLA_ROPE),
             zeros(LANES)]
    out = jnp.concatenate(parts, axis=1).astype(BF16)
    assert out.shape[1] == N_IN_PAD
    return out


def _prep_mla_w(w_uq, w_ukv):
    rq, rkv = w_uq.shape[0], w_ukv.shape[0]
    hd = MLA_NOPE + MLA_ROPE
    wq = jnp.pad(w_uq.reshape(rq, MLA_HEADS, hd), ((0, 0), (0, 0), (0, LANES - hd))).reshape(rq, MLA_HEADS * LANES)
    kv = w_ukv.reshape(rkv, MLA_HEADS, MLA_NOPE + MLA_V)
    wk = jnp.pad(kv[:, :, :MLA_NOPE], ((0, 0), (0, 0), (0, LANES - MLA_NOPE))).reshape(rkv, MLA_HEADS * LANES)
    wv = kv[:, :, MLA_NOPE:].reshape(rkv, MLA_HEADS * MLA_V)
    return wq.astype(BF16), wk.astype(BF16), wv.astype(BF16)


def _axial_angles(n, rot_dim):
    rows = n // GRID_W
    row = jnp.repeat(jnp.arange(rows, dtype=F32), GRID_W)
    col = jnp.tile(jnp.arange(GRID_W, dtype=F32), rows)
    nf = rot_dim // 4
    inv = ROPE_BASE ** (-jnp.arange(nf, dtype=F32) / nf)
    return jnp.concatenate([row[:, None] * inv, col[:, None] * inv], axis=-1)


def _rope_tables(n_lat, n_ctx):
    ang = _axial_angles(n_lat, RET_QK_DIM)
    cos4 = jnp.tile(jnp.cos(ang), (1, RET_HEADS))
    sin4 = jnp.tile(jnp.sin(ang), (1, RET_HEADS))
    cos4 = jnp.concatenate([jnp.ones((n_ctx, LANES), F32), cos4], axis=0)
    sin4 = jnp.concatenate([jnp.zeros((n_ctx, LANES), F32), sin4], axis=0)
    ang = _axial_angles(n_lat, MLA_ROPE)
    c, s = jnp.cos(ang), jnp.sin(ang)
    half = MLA_ROPE // 2
    one = jnp.ones((n_lat, MLA_NOPE), F32)
    zero = lambda w: jnp.zeros((n_lat, w), F32)
    tail = LANES - MLA_NOPE - MLA_ROPE
    ctab = jnp.concatenate([one, c, c, jnp.ones((n_lat, tail), F32)], axis=1)
    s1 = jnp.concatenate([zero(MLA_NOPE + half), s, zero(tail)], axis=1)
    s2 = jnp.concatenate([zero(MLA_NOPE), -s, zero(half + tail)], axis=1)
    ctab = jnp.concatenate([jnp.ones((n_ctx, LANES), F32), ctab], axis=0)
    s1 = jnp.concatenate([jnp.zeros((n_ctx, LANES), F32), s1], axis=0)
    s2 = jnp.concatenate([jnp.zeros((n_ctx, LANES), F32), s2], axis=0)
    return cos4, sin4, ctab, s1, s2


def kernel(x, c, ctx, c_ctx, w_ada, b_ada, norm1_g, norm2_g, w_in, conv_w, conv_b, conv_ln_g, conv_ln_b,
           ssm_conv_w, ssm_conv_b, ssm_dt_bias, ssm_a_log, ssm_d, ssm_norm_g, ret_decay, ret_gn_g, ret_gn_b,
           mla_q_norm_g, mla_kv_norm_g, mla_w_uq, mla_w_ukv, w_branch, w_out, w_ffn_in, w_ffn_out, final_norm_g):
    batch, n_lat, d = x.shape
    n_ctx = ctx.shape[1]
    depth = w_in.shape[0]
    assert d == D_MODEL and n_ctx % 256 == 0 and n_lat % 256 == 0 and batch + 1 <= 8
    cos4, sin4, ctab, s1tab, s2tab = _rope_tables(n_lat, n_ctx)
    xs = jnp.concatenate([ctx, x], axis=1)
    cvec = jnp.zeros((8, d), F32).at[:batch].set(c).at[batch].set(c_ctx)
    for i in range(depth):
        mod = _ada(cvec, w_ada[i], b_ada[i]).reshape(8, 6, d)
        modtab = jnp.zeros((batch, 16, d), F32)
        modtab = modtab.at[:, 0:6].set(jnp.broadcast_to(mod[batch], (batch, 6, d))).at[:, 8:14].set(mod[:batch])
        u = _inproj(xs, modtab, norm1_g[i], _prep_w_in(w_in[i]), n_ctx)
        conv_y = _convmod(u, conv_w[i], conv_b[i], conv_ln_g[i], conv_ln_b[i], n_ctx)
        xa = _ssmconv(u, ssm_conv_w[i], ssm_conv_b[i], n_ctx)
        ssm_f = _ssd(xa, u, ssm_dt_bias[i], ssm_a_log[i], None, None, None, n_ctx, False)
        ssm_y = _ssd(xa, u, ssm_dt_bias[i], ssm_a_log[i], ssm_f, ssm_d[i], ssm_norm_g[i], n_ctx, True)
        ret_f = _ret(u, cos4, sin4, ret_decay[i], None, None, None, n_ctx, False)
        ret_y = _ret(u, cos4, sin4, ret_decay[i], ret_f, ret_gn_g[i], ret_gn_b[i], n_ctx, True)
        wq, wk, wv = _prep_mla_w(mla_w_uq[i], mla_w_ukv[i])
        qt, k, vt = _mlaprep(u, mla_q_norm_g[i], mla_kv_norm_g[i], wq, wk, wv, ctab, s1tab, s2tab)
        att = _flash(k, qt, vt, n_ctx)
        xs = _merge(xs, u, (conv_y, ssm_y, ret_y, att), w_branch[i].astype(BF16), w_out[i].astype(BF16),
                    modtab, n_ctx)
        xs = _ffn(xs, modtab, norm2_g[i], w_ffn_in[i].astype(BF16), w_ffn_out[i].astype(BF16), n_ctx)
    return _final_norm(xs, final_norm_g, n_ctx)
```

```python
import functools
import math

import jax
import jax.numpy as jnp
from jax import lax
from jax.experimental import pallas as pl
from jax.experimental.pallas import tpu as pltpu

F32 = jnp.float32
BF16 = jnp.bfloat16
HIGHEST = lax.Precision.HIGHEST

D_MODEL = 1024
GRID_W = 64
CHUNK = 128
ROPE_BASE = 10000.0
EPS = 1e-6
BRANCH_DIM = D_MODEL // 2
N_BRANCH = 4
CONV_DIM = BRANCH_DIM
CONV_WIDTH = 31
SSM_INNER = BRANCH_DIM
SSM_HEAD_DIM = 64
SSM_HEADS = SSM_INNER // SSM_HEAD_DIM
SSM_GROUPS = 2
SSM_STATE = 128
SSM_CONV = 5
SSM_XBC = SSM_INNER + 2 * SSM_GROUPS * SSM_STATE
RET_HEADS = 4
RET_QK_DIM = 64
RET_INNER = BRANCH_DIM
RET_V_DIM = RET_INNER // RET_HEADS
MLA_HEADS = 8
MLA_NOPE = 64
MLA_ROPE = 32
MLA_V = BRANCH_DIM // MLA_HEADS
MLA_Q_RANK = 384
MLA_KV_RANK = 256
FFN_DIM = ((8 * D_MODEL // 3 + 255) // 256) * 256

LANES = 128
SUBLANES = 8
VMEM_LIMIT_BYTES = 56 * 1024 * 1024

COL_GL = 0
COL_CONV = 4096
COL_XBC = 5120
COL_Z = 6144
COL_RV = 6656
COL_RG = 7168
COL_RQ = 7680
COL_RK = 7936
COL_CKV = 8192
COL_CQ = 8448
COL_DT = 8832
COL_KR = 8960
N_IN_PAD = 9216


def _tile(n, target, mult):
    best = None
    for t in range(mult, min(n, target) + 1, mult):
        if n % t == 0:
            best = t
    assert best is not None, (n, target, mult)
    return best


def _cparams(sem):
    return pltpu.CompilerParams(dimension_semantics=sem, vmem_limit_bytes=VMEM_LIMIT_BYTES)


def _silu(v):
    return v * jax.nn.sigmoid(v)


def _norm_mod(x, mod, g, row0, n_ctx, k):
    y = x * lax.rsqrt(jnp.mean(x * x, axis=-1, keepdims=True) + EPS) * g
    rows = row0 + lax.broadcasted_iota(jnp.int32, (x.shape[0], 1), 0)
    is_ctx = rows < n_ctx
    shift = jnp.where(is_ctx, mod[3 * k:3 * k + 1], mod[8 + 3 * k:9 + 3 * k])
    scale = jnp.where(is_ctx, mod[3 * k + 1:3 * k + 2], mod[9 + 3 * k:10 + 3 * k])
    return y * (1.0 + scale) + shift


def _res_gate(mod, row0, n_rows, n_ctx, k):
    rows = row0 + lax.broadcasted_iota(jnp.int32, (n_rows, 1), 0)
    return jnp.where(rows < n_ctx, mod[3 * k + 2:3 * k + 3], mod[10 + 3 * k:11 + 3 * k])


def _ada_kernel(c_ref, w_ref, b_ref, o_ref):
    o_ref[...] = jnp.dot(_silu(c_ref[...]), w_ref[...], preferred_element_type=F32, precision=HIGHEST) + b_ref[...]


def _ada(cvec, w, b):
    n = w.shape[1]
    tn = _tile(n, 1536, LANES)
    return pl.pallas_call(
        _ada_kernel,
        out_shape=jax.ShapeDtypeStruct((cvec.shape[0], n), F32),
        grid=(n // tn,),
        in_specs=[pl.BlockSpec(cvec.shape, lambda j: (0, 0)),
                  pl.BlockSpec((w.shape[0], tn), lambda j: (0, j)),
                  pl.BlockSpec((1, tn), lambda j: (0, j))],
        out_specs=pl.BlockSpec((cvec.shape[0], tn), lambda j: (0, j)),
        compiler_params=_cparams(("arbitrary",)),
    )(cvec, w, b.reshape(1, n))


def _inproj_kernel(x_ref, mod_ref, g_ref, w_ref, o_ref, h_ref, *, tm, n_ctx):
    i = pl.program_id(1)

    @pl.when(pl.program_id(2) == 0)
    def _():
        h_ref[...] = _norm_mod(x_ref[0], mod_ref[0], g_ref[...], i * tm, n_ctx, 0).astype(BF16)

    o_ref[0] = jnp.dot(h_ref[...], w_ref[...], preferred_element_type=F32)


def _inproj(xs, modtab, g, w, n_ctx):
    b, nt, d = xs.shape
    n = w.shape[1]
    tm = _tile(nt, 1280, 256)
    tn = _tile(n, 1024, LANES)
    return pl.pallas_call(
        functools.partial(_inproj_kernel, tm=tm, n_ctx=n_ctx),
        out_shape=jax.ShapeDtypeStruct((b, nt, n), F32),
        grid=(b, nt // tm, n // tn),
        in_specs=[pl.BlockSpec((1, tm, d), lambda bi, i, j: (bi, i, 0)),
                  pl.BlockSpec((1, 16, d), lambda bi, i, j: (bi, 0, 0)),
                  pl.BlockSpec((1, d), lambda bi, i, j: (0, 0)),
                  pl.BlockSpec((d, tn), lambda bi, i, j: (0, j))],
        out_specs=pl.BlockSpec((1, tm, tn), lambda bi, i, j: (bi, i, j)),
        scratch_shapes=[pltpu.VMEM((tm, d), BF16)],
        compiler_params=_cparams(("parallel", "parallel", "arbitrary")),
    )(xs, modtab, g.reshape(1, d), w)


def _halo_flags(i, t, n_ctx, n_tiles):
    zero_prev = jnp.logical_or(i == 0, i * t == n_ctx)
    zero_next = jnp.logical_or((i + 1) * t == n_ctx, i == n_tiles - 1)
    return zero_prev, zero_next


def _convmod_kernel(cur_ref, prev_ref, next_ref, w_ref, b_ref, lg_ref, lb_ref, o_ref, buf_ref, acc_ref,
                    *, t, n_ctx, n_tiles):
    i = pl.program_id(1)
    zero_prev, zero_next = _halo_flags(i, t, n_ctx, n_tiles)

    def glu(u):
        return u[:, :CONV_DIM] * jax.nn.sigmoid(u[:, CONV_DIM:])

    halo = 16
    buf_ref[0, 0:halo, :] = jnp.where(zero_prev, 0.0, glu(prev_ref[0]))
    buf_ref[0, halo:halo + t, :] = glu(cur_ref[0])
    buf_ref[0, halo + t:2 * halo + t, :] = jnp.where(zero_next, 0.0, glu(next_ref[0]))
    n_sh = t + 2 * halo - SUBLANES
    for sh in range(1, SUBLANES):
        buf_ref[sh, 0:n_sh, :] = buf_ref[0, sh:sh + n_sh, :]
    pad = CONV_WIDTH // 2
    rc = 64
    for r in range(t // rc):
        for c in range(CONV_DIM // LANES):
            cs = slice(c * LANES, (c + 1) * LANES)
            acc = jnp.zeros((rc, LANES), F32)
            for k in range(CONV_WIDTH):
                off = halo + k - pad
                base = r * rc + off - off % SUBLANES
                acc = acc + w_ref[k:k + 1, cs] * buf_ref[off % SUBLANES, base:base + rc, cs]
            acc_ref[r * rc:(r + 1) * rc, cs] = acc + b_ref[:, cs]
    h = acc_ref[...]
    mu = jnp.mean(h, axis=-1, keepdims=True)
    var = jnp.mean(jnp.square(h - mu), axis=-1, keepdims=True)
    y = (h - mu) * lax.rsqrt(var + EPS) * lg_ref[...] + lb_ref[...]
    o_ref[0] = _silu(y)


def _convmod(u, w, bias, ln_g, ln_b, n_ctx):
    b, nt, _ = u.shape
    t = 256
    n_tiles = nt // t
    hb = t // 16
    last_hb = nt // 16 - 1
    width = 2 * CONV_DIM
    cb = COL_CONV // width
    return pl.pallas_call(
        functools.partial(_convmod_kernel, t=t, n_ctx=n_ctx, n_tiles=n_tiles),
        out_shape=jax.ShapeDtypeStruct((b, nt, CONV_DIM), F32),
        grid=(b, n_tiles),
        in_specs=[pl.BlockSpec((1, t, width), lambda bi, i: (bi, i, cb)),
                  pl.BlockSpec((1, 16, width), lambda bi, i: (bi, jnp.maximum(i * hb - 1, 0), cb)),
                  pl.BlockSpec((1, 16, width), lambda bi, i: (bi, jnp.minimum((i + 1) * hb, last_hb), cb)),
                  pl.BlockSpec((CONV_WIDTH, CONV_DIM), lambda bi, i: (0, 0)),
                  pl.BlockSpec((1, CONV_DIM), lambda bi, i: (0, 0)),
                  pl.BlockSpec((1, CONV_DIM), lambda bi, i: (0, 0)),
                  pl.BlockSpec((1, CONV_DIM), lambda bi, i: (0, 0))],
        out_specs=pl.BlockSpec((1, t, CONV_DIM), lambda bi, i: (bi, i, 0)),
        scratch_shapes=[pltpu.VMEM((SUBLANES, t + 32, CONV_DIM), F32), pltpu.VMEM((t, CONV_DIM), F32)],
        compiler_params=_cparams(("parallel", "parallel")),
    )(u, u, u, w, bias.reshape(1, -1), ln_g.reshape(1, -1), ln_b.reshape(1, -1))


def _ssmconv_kernel(cur_ref, prev_ref, next_ref, w_ref, b_ref, o_ref, buf_ref, *, t, n_ctx, n_tiles):
    i = pl.program_id(1)
    zero_prev, zero_next = _halo_flags(i, t, n_ctx, n_tiles)
    halo = 8
    buf_ref[0:halo, :] = jnp.where(zero_prev, 0.0, prev_ref[0])
    buf_ref[halo:halo + t, :] = cur_ref[0]
    buf_ref[halo + t:2 * halo + t, :] = jnp.where(zero_next, 0.0, next_ref[0])
    pad = SSM_CONV // 2
    rc = 64
    for r in range(t // rc):
        for c in range(SSM_XBC // LANES):
            cs = slice(c * LANES, (c + 1) * LANES)
            acc = jnp.zeros((rc, LANES), F32)
            for k in range(SSM_CONV):
                off = halo + r * rc + k - pad
                acc = acc + w_ref[k:k + 1, cs] * buf_ref[off:off + rc, cs]
            o_ref[0, r * rc:(r + 1) * rc, cs] = _silu(acc + b_ref[:, cs])


def _ssmconv(u, w, bias, n_ctx):
    b, nt, _ = u.shape
    t = 256
    n_tiles = nt // t
    hb = t // 8
    last_hb = nt // 8 - 1
    cb = COL_XBC // SSM_XBC
    return pl.pallas_call(
        functools.partial(_ssmconv_kernel, t=t, n_ctx=n_ctx, n_tiles=n_tiles),
        out_shape=jax.ShapeDtypeStruct((b, nt, SSM_XBC), F32),
        grid=(b, n_tiles),
        in_specs=[pl.BlockSpec((1, t, SSM_XBC), lambda bi, i: (bi, i, cb)),
                  pl.BlockSpec((1, 8, SSM_XBC), lambda bi, i: (bi, jnp.maximum(i * hb - 1, 0), cb)),
                  pl.BlockSpec((1, 8, SSM_XBC), lambda bi, i: (bi, jnp.minimum((i + 1) * hb, last_hb), cb)),
                  pl.BlockSpec((SSM_CONV, SSM_XBC), lambda bi, i: (0, 0)),
                  pl.BlockSpec((1, SSM_XBC), lambda bi, i: (0, 0))],
        out_specs=pl.BlockSpec((1, t, SSM_XBC), lambda bi, i: (bi, i, 0)),
        scratch_shapes=[pltpu.VMEM((t + 16, SSM_XBC), F32)],
        compiler_params=_cparams(("parallel", "parallel")),
    )(u, u, u, w, bias.reshape(1, -1))


SCAN_CHUNKS = 2


def _chunk_index(s, reverse, n_chunks, n_ctx_chunks):
    if not reverse:
        return s
    return jnp.where(s < n_ctx_chunks, n_ctx_chunks - 1 - s, n_chunks + n_ctx_chunks - 1 - s)


def _softplus(v):
    return jnp.maximum(v, 0.0) + jnp.log(1.0 + jnp.exp(-jnp.abs(v)))


def _ssd_kernel(*refs, reverse):
    h_ref = refs[-1]

    @pl.when(pl.program_id(1) == 0)
    def _():
        h_ref[...] = jnp.zeros_like(h_ref)

    order = range(SCAN_CHUNKS - 1, -1, -1) if reverse else range(SCAN_CHUNKS)
    for ci in order:
        _ssd_chunk(refs, slice(ci * CHUNK, (ci + 1) * CHUNK), reverse)


def _ssd_chunk(refs, rows, reverse):
    if reverse:
        xa_ref, dt_ref, dtb_ref, alog_ref, yf_ref, z_ref, dskip_ref, ng_ref, o_ref, h_ref = refs
    else:
        xa_ref, dt_ref, dtb_ref, alog_ref, o_ref, h_ref = refs
    L = CHUNK
    xa = xa_ref[0, rows, :]
    dt = _softplus(dt_ref[0, rows, :] + dtb_ref[...])
    la = dt * (-jnp.exp(alog_ref[...]))
    r = lax.broadcasted_iota(jnp.int32, (L, L), 0)
    c = lax.broadcasted_iota(jnp.int32, (L, L), 1)
    keep = (r <= c) if reverse else (r >= c)
    cum = jnp.dot(keep.astype(F32), la, preferred_element_type=F32, precision=HIGHEST)
    cum_t = cum.T
    off = SSM_HEADS if reverse else 0
    tot_row = 0 if reverse else L - 1
    hpg = SSM_HEADS // SSM_GROUPS
    gw = hpg * SSM_HEAD_DIM
    lane_head = lax.broadcasted_iota(jnp.int32, (L, gw), 1) // SSM_HEAD_DIM
    ys = []
    for g in range(SSM_GROUPS):
        b_f = xa[:, SSM_INNER + g * SSM_STATE:SSM_INNER + (g + 1) * SSM_STATE]
        c_b = xa[:, SSM_INNER + (SSM_GROUPS + g) * SSM_STATE:SSM_INNER + (SSM_GROUPS + g + 1) * SSM_STATE].astype(BF16)
        gram = lax.dot_general(c_b, b_f.astype(BF16), (((1,), (1,)), ((), ())), preferred_element_type=F32)
        b_t = b_f.T.astype(BF16)

        def expand(mat):
            out = jnp.broadcast_to(mat[:, off + g * hpg + hpg - 1:off + g * hpg + hpg], (L, gw))
            for j in range(hpg - 2, -1, -1):
                out = jnp.where(lane_head == j, mat[:, off + g * hpg + j:off + g * hpg + j + 1], out)
            return out

        dt_l = expand(dt)
        cum_l = expand(cum)
        tot_l = cum_l[tot_row:tot_row + 1, :]
        xdt = xa[:, g * gw:(g + 1) * gw] * dt_l
        xdt_b = xdt.astype(BF16)
        ms = []
        for j in range(hpg):
            hh = off + g * hpg + j
            seg = cum[:, hh:hh + 1] - cum_t[hh:hh + 1, :]
            dec = jnp.exp(jnp.where(keep, seg, -jnp.inf))
            ms.append((gram * dec).astype(BF16))
        rr = jnp.dot(jnp.concatenate(ms, axis=0), xdt_b, preferred_element_type=F32)
        y = jnp.where(lane_head == 0, rr[0:L], 0.0)
        for j in range(1, hpg):
            y = y + jnp.where(lane_head == j, rr[j * L:(j + 1) * L], 0.0)
        hg = h_ref[g]
        y = y + jnp.dot(c_b, hg.astype(BF16), preferred_element_type=F32) * jnp.exp(cum_l)
        wm = jnp.exp(tot_l - cum_l)
        h_ref[g] = jnp.exp(tot_l) * hg + jnp.dot(b_t, (xdt * wm).astype(BF16), preferred_element_type=F32)
        ys.append(y)
    y = jnp.concatenate(ys, axis=1)
    if not reverse:
        o_ref[0, rows, :] = y
        return
    y = yf_ref[0, rows, :] + y + dskip_ref[...] * xa[:, :SSM_INNER]
    y = y * _silu(z_ref[0, rows, :])
    outs = []
    for g in range(SSM_GROUPS):
        yy = y[:, g * gw:(g + 1) * gw]
        outs.append(yy * lax.rsqrt(jnp.mean(yy * yy, axis=-1, keepdims=True) + EPS) * ng_ref[:, g * gw:(g + 1) * gw])
    o_ref[0, rows, :] = jnp.concatenate(outs, axis=1)


def _ssd(xa, u, dt_bias, a_log, yf, d_skip, norm_g, n_ctx, reverse):
    b, nt, _ = xa.shape
    blk = SCAN_CHUNKS * CHUNK
    nc = nt // blk
    ncc = n_ctx // blk
    cidx = functools.partial(_chunk_index, reverse=reverse, n_chunks=nc, n_ctx_chunks=ncc)
    pad16 = lambda p: jnp.pad(p.reshape(1, -1), ((0, 0), (0, LANES - 2 * SSM_HEADS)))
    in_specs = [pl.BlockSpec((1, blk, SSM_XBC), lambda bi, s: (bi, cidx(s), 0)),
                pl.BlockSpec((1, blk, LANES), lambda bi, s: (bi, cidx(s), COL_DT // LANES)),
                pl.BlockSpec((1, LANES), lambda bi, s: (0, 0)),
                pl.BlockSpec((1, LANES), lambda bi, s: (0, 0))]
    args = [xa, u, pad16(dt_bias), pad16(a_log)]
    if reverse:
        in_specs += [pl.BlockSpec((1, blk, SSM_INNER), lambda bi, s: (bi, cidx(s), 0)),
                     pl.BlockSpec((1, blk, SSM_INNER), lambda bi, s: (bi, cidx(s), COL_Z // SSM_INNER)),
                     pl.BlockSpec((1, SSM_INNER), lambda bi, s: (0, 0)),
                     pl.BlockSpec((1, SSM_INNER), lambda bi, s: (0, 0))]
        args += [yf, u, jnp.repeat(d_skip, SSM_HEAD_DIM).reshape(1, -1), norm_g.reshape(1, -1)]
    return pl.pallas_call(
        functools.partial(_ssd_kernel, reverse=reverse),
        out_shape=jax.ShapeDtypeStruct((b, nt, SSM_INNER), F32),
        grid=(b, nc),
        in_specs=in_specs,
        out_specs=pl.BlockSpec((1, blk, SSM_INNER), lambda bi, s: (bi, cidx(s), 0)),
        scratch_shapes=[pltpu.VMEM((SSM_GROUPS, SSM_STATE, SSM_INNER // SSM_GROUPS), F32)],
        compiler_params=_cparams(("parallel", "arbitrary")),
    )(*args)


def _ret_rope(v, cos, sin):
    lo, hi = v[:, :LANES], v[:, LANES:]
    return jnp.concatenate([lo * cos - hi * sin, lo * sin + hi * cos], axis=1)


def _ret_kernel(*refs, reverse):
    h_ref = refs[-1]

    @pl.when(pl.program_id(1) == 0)
    def _():
        h_ref[...] = jnp.zeros_like(h_ref)

    order = range(SCAN_CHUNKS - 1, -1, -1) if reverse else range(SCAN_CHUNKS)
    for ci in order:
        _ret_chunk(refs, slice(ci * CHUNK, (ci + 1) * CHUNK), reverse)


def _ret_chunk(refs, rows, reverse):
    if reverse:
        (q_ref, k_ref, v_ref, cos_ref, sin_ref, dec_ref, yf_ref, g_ref, gng_ref, gnb_ref, o_ref, h_ref) = refs
    else:
        (q_ref, k_ref, v_ref, cos_ref, sin_ref, dec_ref, decb_ref, o_ref, h_ref) = refs
    L = CHUNK
    qk = RET_HEADS * RET_QK_DIM
    cos, sin = cos_ref[rows, :], sin_ref[rows, :]
    q = _ret_rope(q_ref[0, rows, :], cos, sin)
    k = _ret_rope(k_ref[0, rows, :], cos, sin) * (RET_QK_DIM ** -0.5)
    v = v_ref[0, rows, :]
    q_b = q.astype(BF16)
    k_t = k.T.astype(BF16)
    lg = -jnp.exp(dec_ref[...])
    t = lax.broadcasted_iota(jnp.int32, (L, 1), 0).astype(F32)
    if reverse:
        e_in = jnp.exp(lg * (L - t))
        w_st = jnp.exp(lg * t)
    else:
        e_in = jnp.exp(lg * (t + 1.0))
        w_st = jnp.exp(lg * (L - 1.0 - t))
    hs = h_ref[...]
    y = jnp.dot(q_b, hs.astype(BF16), preferred_element_type=F32) * e_in
    upd = jnp.dot(k_t, (v * w_st).astype(BF16), preferred_element_type=F32)
    row_head = (lax.broadcasted_iota(jnp.int32, (qk, RET_INNER), 0) % LANES) // (RET_QK_DIM // 2)
    lane_head = lax.broadcasted_iota(jnp.int32, (qk, RET_INNER), 1) // RET_V_DIM
    h_ref[...] = jnp.exp(lg * float(L)) * hs + jnp.where(row_head == lane_head, upd, 0.0)

    if not reverse:
        lgb = -jnp.exp(decb_ref[...])
        q_head = (lax.broadcasted_iota(jnp.int32, (L, qk), 1) % LANES) // (RET_QK_DIM // 2)
        qs = jnp.concatenate([jnp.where(q_head == h, q, 0.0) for h in range(RET_HEADS)], axis=0).astype(BF16)
        sc = jnp.dot(qs, k_t, preferred_element_type=F32)
        dl = (lax.broadcasted_iota(jnp.int32, (L, L), 0) - lax.broadcasted_iota(jnp.int32, (L, L), 1)).astype(F32)
        parts = []
        for h in range(RET_HEADS):
            vs = slice(h * RET_V_DIM, (h + 1) * RET_V_DIM)
            dcomb = jnp.where(dl > 0, jnp.exp(lg[:, vs] * jnp.maximum(dl, 0.0)),
                              jnp.where(dl < 0, jnp.exp(lgb[:, vs] * jnp.maximum(-dl, 0.0)), 2.0))
            p = (sc[h * L:(h + 1) * L] * dcomb).astype(BF16)
            parts.append(jnp.dot(p, v[:, vs].astype(BF16), preferred_element_type=F32))
        o_ref[0, rows, :] = y + jnp.concatenate(parts, axis=1)
        return
    y = yf_ref[0, rows, :] + y
    outs = []
    for h in range(RET_HEADS):
        vs = slice(h * RET_V_DIM, (h + 1) * RET_V_DIM)
        yy = y[:, vs]
        mu = jnp.mean(yy, axis=-1, keepdims=True)
        var = jnp.mean(jnp.square(yy - mu), axis=-1, keepdims=True)
        outs.append((yy - mu) * lax.rsqrt(var + EPS) * gng_ref[:, vs] + gnb_ref[:, vs])
    o_ref[0, rows, :] = _silu(g_ref[0, rows, :]) * jnp.concatenate(outs, axis=1)


def _ret(u, cos4, sin4, ret_decay, yf, gn_g, gn_b, n_ctx, reverse):
    b, nt, _ = u.shape
    blk = SCAN_CHUNKS * CHUNK
    nc = nt // blk
    ncc = n_ctx // blk
    cidx = functools.partial(_chunk_index, reverse=reverse, n_chunks=nc, n_ctx_chunks=ncc)
    qk = RET_HEADS * RET_QK_DIM
    lane_dec = lambda d: jnp.repeat(d, RET_V_DIM).reshape(1, -1)
    in_specs = [pl.BlockSpec((1, blk, qk), lambda bi, s: (bi, cidx(s), COL_RQ // qk)),
                pl.BlockSpec((1, blk, qk), lambda bi, s: (bi, cidx(s), COL_RK // qk)),
                pl.BlockSpec((1, blk, RET_INNER), lambda bi, s: (bi, cidx(s), COL_RV // RET_INNER)),
                pl.BlockSpec((blk, LANES), lambda bi, s: (cidx(s), 0)),
                pl.BlockSpec((blk, LANES), lambda bi, s: (cidx(s), 0)),
                pl.BlockSpec((1, RET_INNER), lambda bi, s: (0, 0))]
    args = [u, u, u, cos4, sin4, lane_dec(ret_decay[1] if reverse else ret_decay[0])]
    if reverse:
        in_specs += [pl.BlockSpec((1, blk, RET_INNER), lambda bi, s: (bi, cidx(s), 0)),
                     pl.BlockSpec((1, blk, RET_INNER), lambda bi, s: (bi, cidx(s), COL_RG // RET_INNER)),
                     pl.BlockSpec((1, RET_INNER), lambda bi, s: (0, 0)),
                     pl.BlockSpec((1, RET_INNER), lambda bi, s: (0, 0))]
        args += [yf, u, gn_g.reshape(1, -1), gn_b.reshape(1, -1)]
    else:
        in_specs += [pl.BlockSpec((1, RET_INNER), lambda bi, s: (0, 0))]
        args += [lane_dec(ret_decay[1])]
    return pl.pallas_call(
        functools.partial(_ret_kernel, reverse=reverse),
        out_shape=jax.ShapeDtypeStruct((b, nt, RET_INNER), F32),
        grid=(b, nc),
        in_specs=in_specs,
        out_specs=pl.BlockSpec((1, blk, RET_INNER), lambda bi, s: (bi, cidx(s), 0)),
        scratch_shapes=[pltpu.VMEM((qk, RET_INNER), F32)],
        compiler_params=_cparams(("parallel", "arbitrary")),
    )(*args)


def _mla_rope(v, c, s1, s2):
    return v * c + pltpu.roll(v, 16, 1) * s1 + pltpu.roll(v, LANES - 16, 1) * s2


def _rms(v, g):
    return v * lax.rsqrt(jnp.mean(v * v, axis=-1, keepdims=True) + EPS) * g


def _mlaprep_kernel(cq_ref, ckv_ref, kr_ref, qg_ref, kvg_ref, wq_ref, wk_ref, wv_ref, c_ref, s1_ref, s2_ref,
                    qt_out, k_out, vt_out):
    c, s1, s2 = c_ref[...], s1_ref[...], s2_ref[...]
    scale = (MLA_NOPE + MLA_ROPE) ** -0.5 * math.log2(math.e)
    q = jnp.dot(_rms(cq_ref[0], qg_ref[...]).astype(BF16), wq_ref[...], preferred_element_type=F32)
    ckv = _rms(ckv_ref[0], kvg_ref[...]).astype(BF16)
    kn = jnp.dot(ckv, wk_ref[...], preferred_element_type=F32)
    vt_out[0] = jnp.dot(ckv, wv_ref[...], preferred_element_type=F32).T.astype(BF16)
    krr = _mla_rope(kr_ref[0], c, s1, s2)
    for h in range(MLA_HEADS):
        hs = slice(h * LANES, (h + 1) * LANES)
        qt_out[0, hs, :] = (_mla_rope(q[:, hs], c, s1, s2) * scale).T.astype(BF16)
        k_out[0, :, hs] = (kn[:, hs] + krr).astype(BF16)


def _mlaprep(u, q_g, kv_g, wq, wk, wv, ctab, s1tab, s2tab):
    b, nt, _ = u.shape
    tm = _tile(nt, 1280, 256)
    hw = MLA_HEADS * LANES
    const = lambda shape: pl.BlockSpec(shape, lambda bi, i: (0, 0))
    return pl.pallas_call(
        _mlaprep_kernel,
        out_shape=(jax.ShapeDtypeStruct((b, hw, nt), BF16), jax.ShapeDtypeStruct((b, nt, hw), BF16),
                   jax.ShapeDtypeStruct((b, MLA_HEADS * MLA_V, nt), BF16)),
        grid=(b, nt // tm),
        in_specs=[pl.BlockSpec((1, tm, MLA_Q_RANK), lambda bi, i: (bi, i, COL_CQ // MLA_Q_RANK)),
                  pl.BlockSpec((1, tm, MLA_KV_RANK), lambda bi, i: (bi, i, COL_CKV // MLA_KV_RANK)),
                  pl.BlockSpec((1, tm, LANES), lambda bi, i: (bi, i, COL_KR // LANES)),
                  const((1, MLA_Q_RANK)), const((1, MLA_KV_RANK)),
                  const((MLA_Q_RANK, hw)), const((MLA_KV_RANK, hw)), const((MLA_KV_RANK, MLA_HEADS * MLA_V)),
                  pl.BlockSpec((tm, LANES), lambda bi, i: (i, 0)),
                  pl.BlockSpec((tm, LANES), lambda bi, i: (i, 0)),
                  pl.BlockSpec((tm, LANES), lambda bi, i: (i, 0))],
        out_specs=(pl.BlockSpec((1, hw, tm), lambda bi, i: (bi, 0, i)),
                   pl.BlockSpec((1, tm, hw), lambda bi, i: (bi, i, 0)),
                   pl.BlockSpec((1, MLA_HEADS * MLA_V, tm), lambda bi, i: (bi, 0, i))),
        compiler_params=_cparams(("parallel", "parallel")),
    )(u, u, u, q_g.reshape(1, -1), kv_g.reshape(1, -1), wq, wk, wv, ctab, s1tab, s2tab)


ACC_ROWS = MLA_V + 16


def _flash_kernel(k_ref, qt_ref, vt_ref, o_ref, m_ref, acc_ref, *, tq, tk, n_ctx, nk):
    i = pl.program_id(2)
    j = pl.program_id(3)

    @pl.when(j == 0)
    def _():
        m_ref[...] = jnp.full_like(m_ref, -jnp.inf)
        acc_ref[...] = jnp.zeros_like(acc_ref)

    def step(masked):
        ones = jnp.ones((ACC_ROWS - MLA_V, tk), BF16)
        for hh in range(2):
            hs = slice(hh * LANES, (hh + 1) * LANES)
            s = jnp.dot(k_ref[0, :, hs], qt_ref[0, hs, :], preferred_element_type=F32)
            if masked:
                keys = j * tk + lax.broadcasted_iota(jnp.int32, (tk, n_ctx), 0)
                s_ctx = jnp.where(keys >= n_ctx, -1e30, s[:, :n_ctx])
                s = s_ctx if tq == n_ctx else jnp.concatenate([s_ctx, s[:, n_ctx:]], axis=1)
            m_prev = m_ref[hh]
            m_new = jnp.maximum(m_prev, jnp.max(s, axis=0, keepdims=True))
            alpha = jnp.exp2(m_prev - m_new)
            p = jnp.exp2(s - m_new).astype(BF16)
            vs = slice(hh * MLA_V, (hh + 1) * MLA_V)
            lhs = jnp.concatenate([vt_ref[0, vs, :], ones], axis=0)
            acc_ref[hh] = alpha * acc_ref[hh] + jnp.dot(lhs, p, preferred_element_type=F32)
            m_ref[hh] = m_new

    @pl.when(i * tq < n_ctx)
    def _():
        step(True)

    @pl.when(i * tq >= n_ctx)
    def _():
        step(False)

    @pl.when(j == nk - 1)
    def _():
        outs = []
        for hh in range(2):
            a = acc_ref[hh]
            outs.append(a[:MLA_V] * (1.0 / a[MLA_V:MLA_V + 1]))
        o_ref[0] = jnp.concatenate(outs, axis=0).T


def _flash(k, qt, vt, n_ctx, tq_target=3328, tk_target=1280):
    b, nt, _ = k.shape
    tq = _tile(nt, tq_target, LANES)
    tk = _tile(nt, tk_target, LANES)
    assert tq >= n_ctx and n_ctx % LANES == 0
    nk = nt // tk
    pairs = MLA_HEADS // 2
    return pl.pallas_call(
        functools.partial(_flash_kernel, tq=tq, tk=tk, n_ctx=n_ctx, nk=nk),
        out_shape=jax.ShapeDtypeStruct((b, nt, MLA_HEADS * MLA_V), F32),
        grid=(b, pairs, nt // tq, nk),
        in_specs=[pl.BlockSpec((1, tk, 2 * LANES), lambda bi, p, i, j: (bi, j, p)),
                  pl.BlockSpec((1, 2 * LANES, tq), lambda bi, p, i, j: (bi, p, i)),
                  pl.BlockSpec((1, 2 * MLA_V, tk), lambda bi, p, i, j: (bi, p, j))],
        out_specs=pl.BlockSpec((1, tq, 2 * MLA_V), lambda bi, p, i, j: (bi, i, p)),
        scratch_shapes=[pltpu.VMEM((2, 1, tq), F32), pltpu.VMEM((2, ACC_ROWS, tq), F32)],
        compiler_params=_cparams(("parallel", "parallel", "parallel", "arbitrary")),
    )(k, qt, vt)


def _merge_kernel(xs_ref, gl_ref, b0_ref, b1_ref, b2_ref, b3_ref, wb_ref, wo_ref, mod_ref, o_ref, *, tm, n_ctx):
    i = pl.program_id(1)
    merged = None
    for n, br in enumerate((b0_ref, b1_ref, b2_ref, b3_ref)):
        proj = jnp.dot(br[0].astype(BF16), wb_ref[n], preferred_element_type=F32)
        term = jax.nn.sigmoid(gl_ref[0, :, n * D_MODEL:(n + 1) * D_MODEL]) * proj
        merged = term if merged is None else merged + term
    out = jnp.dot(merged.astype(BF16), wo_ref[...], preferred_element_type=F32)
    o_ref[0] = xs_ref[0] + _res_gate(mod_ref[0], i * tm, tm, n_ctx, 0) * out


def _merge(xs, u, branches, wb, wo, modtab, n_ctx):
    b, nt, d = xs.shape
    tm = 256
    row = lambda w: pl.BlockSpec((1, tm, w), lambda bi, i: (bi, i, 0))
    return pl.pallas_call(
        functools.partial(_merge_kernel, tm=tm, n_ctx=n_ctx),
        out_shape=jax.ShapeDtypeStruct((b, nt, d), F32),
        grid=(b, nt // tm),
        in_specs=[row(d), row(N_BRANCH * d), row(BRANCH_DIM), row(BRANCH_DIM), row(BRANCH_DIM), row(BRANCH_DIM),
                  pl.BlockSpec((N_BRANCH, BRANCH_DIM, d), lambda bi, i: (0, 0, 0)),
                  pl.BlockSpec((d, d), lambda bi, i: (0, 0)),
                  pl.BlockSpec((1, 16, d), lambda bi, i: (bi, 0, 0))],
        out_specs=row(d),
        compiler_params=_cparams(("parallel", "parallel")),
    )(xs, u, *branches, wb, wo, modtab)


def _ffn_kernel(xs_ref, mod_ref, g_ref, wa_ref, wg_ref, wo_ref, o_ref, h_ref, acc_ref, *, tm, n_ctx, nj):
    i = pl.program_id(1)
    j = pl.program_id(2)

    @pl.when(j == 0)
    def _():
        h_ref[...] = _norm_mod(xs_ref[0], mod_ref[0], g_ref[...], i * tm, n_ctx, 1).astype(BF16)

    h = h_ref[...]
    a = jnp.dot(h, wa_ref[...], preferred_element_type=F32)
    gate = jnp.dot(h, wg_ref[...], preferred_element_type=F32)
    part = jnp.dot((_silu(gate) * a).astype(BF16), wo_ref[...], preferred_element_type=F32)

    @pl.when(j == 0)
    def _():
        acc_ref[...] = part

    @pl.when(j > 0)
    def _():
        acc_ref[...] += part

    @pl.when(j == nj - 1)
    def _():
        o_ref[0] = xs_ref[0] + _res_gate(mod_ref[0], i * tm, tm, n_ctx, 1) * acc_ref[...]


def _ffn(xs, modtab, g, w_in, w_out, n_ctx):
    b, nt, d = xs.shape
    f = w_out.shape[0]
    tm = _tile(nt, 640, 128)
    tf = _tile(f, 1408, LANES)
    nj = f // tf
    return pl.pallas_call(
        functools.partial(_ffn_kernel, tm=tm, n_ctx=n_ctx, nj=nj),
        out_shape=jax.ShapeDtypeStruct((b, nt, d), F32),
        grid=(b, nt // tm, nj),
        in_specs=[pl.BlockSpec((1, tm, d), lambda bi, i, j: (bi, i, 0)),
                  pl.BlockSpec((1, 16, d), lambda bi, i, j: (bi, 0, 0)),
                  pl.BlockSpec((1, d), lambda bi, i, j: (0, 0)),
                  pl.BlockSpec((d, tf), lambda bi, i, j: (0, j)),
                  pl.BlockSpec((d, tf), lambda bi, i, j: (0, nj + j)),
                  pl.BlockSpec((tf, d), lambda bi, i, j: (j, 0))],
        out_specs=pl.BlockSpec((1, tm, d), lambda bi, i, j: (bi, i, 0)),
        scratch_shapes=[pltpu.VMEM((tm, d), BF16), pltpu.VMEM((tm, d), F32)],
        compiler_params=_cparams(("parallel", "parallel", "arbitrary")),
    )(xs, modtab, g.reshape(1, d), w_in, w_in, w_out)


def _final_kernel(x_ref, g_ref, o_ref):
    o_ref[0] = _rms(x_ref[0], g_ref[...])


def _final_norm(xs, g, n_ctx):
    b, nt, d = xs.shape
    t = 256
    skip = n_ctx // t
    return pl.pallas_call(
        _final_kernel,
        out_shape=jax.ShapeDtypeStruct((b, nt - n_ctx, d), F32),
        grid=(b, (nt - n_ctx) // t),
        in_specs=[pl.BlockSpec((1, t, d), lambda bi, i: (bi, i + skip, 0)),
                  pl.BlockSpec((1, d), lambda bi, i: (0, 0))],
        out_specs=pl.BlockSpec((1, t, d), lambda bi, i: (bi, i, 0)),
        compiler_params=_cparams(("parallel", "parallel")),
    )(xs, g.reshape(1, d))


def _prep_w_in(w):
    d = w.shape[0]
    o_conv, o_z, o_xbc, o_dt = 0, 1024, 1536, 2560
    o_rq, o_rk, o_rv, o_rg = 2576, 2832, 3088, 3600
    o_cq, o_ckv, o_kr, o_gl = 4112, 4496, 4752, 4784
    seg = lambda o, n: w[:, o:o + n]

    def halves(o):
        s = seg(o, RET_HEADS * RET_QK_DIM).reshape(d, RET_HEADS, 2, RET_QK_DIM // 2)
        return jnp.transpose(s, (0, 2, 1, 3)).reshape(d, RET_HEADS * RET_QK_DIM)

    zeros = lambda n: jnp.zeros((d, n), w.dtype)
    parts = [seg(o_gl, N_BRANCH * D_MODEL), seg(o_conv, 2 * CONV_DIM), seg(o_xbc, SSM_XBC), seg(o_z, SSM_INNER),
             seg(o_rv, RET_INNER), seg(o_rg, RET_INNER), halves(o_rq), halves(o_rk),
             seg(o_ckv, MLA_KV_RANK), seg(o_cq, MLA_Q_RANK),
             seg(o_dt, 2 * SSM_HEADS), zeros(LANES - 2 * SSM_HEADS),
             zeros(MLA_NOPE), seg(o_kr, MLA_ROPE), zeros(LANES - MLA_NOPE - MLA_ROPE),
             zeros(LANES)]
    out = jnp.concatenate(parts, axis=1).astype(BF16)
    assert out.shape[1] == N_IN_PAD
    return out


def _prep_mla_w(w_uq, w_ukv):
    rq, rkv = w_uq.shape[0], w_ukv.shape[0]
    hd = MLA_NOPE + MLA_ROPE
    wq = jnp.pad(w_uq.reshape(rq, MLA_HEADS, hd), ((0, 0), (0, 0), (0, LANES - hd))).reshape(rq, MLA_HEADS * LANES)
    kv = w_ukv.reshape(rkv, MLA_HEADS, MLA_NOPE + MLA_V)
    wk = jnp.pad(kv[:, :, :MLA_NOPE], ((0, 0), (0, 0), (0, LANES - MLA_NOPE))).reshape(rkv, MLA_HEADS * LANES)
    wv = kv[:, :, MLA_NOPE:].reshape(rkv, MLA_HEADS * MLA_V)
    return wq.astype(BF16), wk.astype(BF16), wv.astype(BF16)


def _axial_angles(n, rot_dim):
    rows = n // GRID_W
    row = jnp.repeat(jnp.arange(rows, dtype=F32), GRID_W)
    col = jnp.tile(jnp.arange(GRID_W, dtype=F32), rows)
    nf = rot_dim // 4
    inv = ROPE_BASE ** (-jnp.arange(nf, dtype=F32) / nf)
    return jnp.concatenate([row[:, None] * inv, col[:, None] * inv], axis=-1)


def _rope_tables(n_lat, n_ctx):
    ang = _axial_angles(n_lat, RET_QK_DIM)
    cos4 = jnp.tile(jnp.cos(ang), (1, RET_HEADS))
    sin4 = jnp.tile(jnp.sin(ang), (1, RET_HEADS))
    cos4 = jnp.concatenate([jnp.ones((n_ctx, LANES), F32), cos4], axis=0)
    sin4 = jnp.concatenate([jnp.zeros((n_ctx, LANES), F32), sin4], axis=0)
    ang = _axial_angles(n_lat, MLA_ROPE)
    c, s = jnp.cos(ang), jnp.sin(ang)
    half = MLA_ROPE // 2
    one = jnp.ones((n_lat, MLA_NOPE), F32)
    zero = lambda w: jnp.zeros((n_lat, w), F32)
    tail = LANES - MLA_NOPE - MLA_ROPE
    ctab = jnp.concatenate([one, c, c, jnp.ones((n_lat, tail), F32)], axis=1)
    s1 = jnp.concatenate([zero(MLA_NOPE + half), s, zero(tail)], axis=1)
    s2 = jnp.concatenate([zero(MLA_NOPE), -s, zero(half + tail)], axis=1)
    ctab = jnp.concatenate([jnp.ones((n_ctx, LANES), F32), ctab], axis=0)
    s1 = jnp.concatenate([jnp.zeros((n_ctx, LANES), F32), s1], axis=0)
    s2 = jnp.concatenate([jnp.zeros((n_ctx, LANES), F32), s2], axis=0)
    return cos4, sin4, ctab, s1, s2


def kernel(x, c, ctx, c_ctx, w_ada, b_ada, norm1_g, norm2_g, w_in, conv_w, conv_b, conv_ln_g, conv_ln_b,
           ssm_conv_w, ssm_conv_b, ssm_dt_bias, ssm_a_log, ssm_d, ssm_norm_g, ret_decay, ret_gn_g, ret_gn_b,
           mla_q_norm_g, mla_kv_norm_g, mla_w_uq, mla_w_ukv, w_branch, w_out, w_ffn_in, w_ffn_out, final_norm_g):
    batch, n_lat, d = x.shape
    n_ctx = ctx.shape[1]
    depth = w_in.shape[0]
    assert d == D_MODEL and n_ctx % 256 == 0 and n_lat % 256 == 0 and batch + 1 <= 8
    assert n_ctx % (SCAN_CHUNKS * CHUNK) == 0
    cos4, sin4, ctab, s1tab, s2tab = _rope_tables(n_lat, n_ctx)
    xs = jnp.concatenate([ctx, x], axis=1)
    cvec = jnp.zeros((8, d), F32).at[:batch].set(c).at[batch].set(c_ctx)
    for i in range(depth):
        mod = _ada(cvec, w_ada[i], b_ada[i]).reshape(8, 6, d)
        modtab = jnp.zeros((batch, 16, d), F32)
        modtab = modtab.at[:, 0:6].set(jnp.broadcast_to(mod[batch], (batch, 6, d))).at[:, 8:14].set(mod[:batch])
        u = _inproj(xs, modtab, norm1_g[i], _prep_w_in(w_in[i]), n_ctx)
        conv_y = _convmod(u, conv_w[i], conv_b[i], conv_ln_g[i], conv_ln_b[i], n_ctx)
        xa = _ssmconv(u, ssm_conv_w[i], ssm_conv_b[i], n_ctx)
        ssm_f = _ssd(xa, u, ssm_dt_bias[i], ssm_a_log[i], None, None, None, n_ctx, False)
        ssm_y = _ssd(xa, u, ssm_dt_bias[i], ssm_a_log[i], ssm_f, ssm_d[i], ssm_norm_g[i], n_ctx, True)
        ret_f = _ret(u, cos4, sin4, ret_decay[i], None, None, None, n_ctx, False)
        ret_y = _ret(u, cos4, sin4, ret_decay[i], ret_f, ret_gn_g[i], ret_gn_b[i], n_ctx, True)
        wq, wk, wv = _prep_mla_w(mla_w_uq[i], mla_w_ukv[i])
        qt, k, vt = _mlaprep(u, mla_q_norm_g[i], mla_kv_norm_g[i], wq, wk, wv, ctab, s1tab, s2tab)
        att = _flash(k, qt, vt, n_ctx)
        xs = _merge(xs, u, (conv_y, ssm_y, ret_y, att), w_branch[i].astype(BF16), w_out[i].astype(BF16),
                    modtab, n_ctx)
        xs = _ffn(xs, modtab, norm2_g[i], w_ffn_in[i].astype(BF16), w_ffn_out[i].astype(BF16), n_ctx)
    return _final_norm(xs, final_norm_g, n_ctx)
```

```python
import functools
import math

import jax
import jax.numpy as jnp
from jax import lax
from jax.experimental import pallas as pl
from jax.experimental.pallas import tpu as pltpu

F32 = jnp.float32
BF16 = jnp.bfloat16
HIGHEST = lax.Precision.HIGHEST

D_MODEL = 1024
GRID_W = 64
CHUNK = 128
ROPE_BASE = 10000.0
EPS = 1e-6
BRANCH_DIM = D_MODEL // 2
N_BRANCH = 4
CONV_DIM = BRANCH_DIM
CONV_WIDTH = 31
SSM_INNER = BRANCH_DIM
SSM_HEAD_DIM = 64
SSM_HEADS = SSM_INNER // SSM_HEAD_DIM
SSM_GROUPS = 2
SSM_STATE = 128
SSM_CONV = 5
SSM_XBC = SSM_INNER + 2 * SSM_GROUPS * SSM_STATE
RET_HEADS = 4
RET_QK_DIM = 64
RET_INNER = BRANCH_DIM
RET_V_DIM = RET_INNER // RET_HEADS
MLA_HEADS = 8
MLA_NOPE = 64
MLA_ROPE = 32
MLA_V = BRANCH_DIM // MLA_HEADS
MLA_Q_RANK = 384
MLA_KV_RANK = 256
FFN_DIM = ((8 * D_MODEL // 3 + 255) // 256) * 256

LANES = 128
SUBLANES = 8
VMEM_LIMIT_BYTES = 56 * 1024 * 1024

COL_GL = 0
COL_CONV = 4096
COL_XBC = 5120
COL_Z = 6144
COL_RV = 6656
COL_RG = 7168
COL_RQ = 7680
COL_RK = 7936
COL_CKV = 8192
COL_CQ = 8448
COL_DT = 8832
COL_KR = 8960
N_IN_PAD = 9216


def _tile(n, target, mult):
    best = None
    for t in range(mult, min(n, target) + 1, mult):
        if n % t == 0:
            best = t
    assert best is not None, (n, target, mult)
    return best


def _cparams(sem):
    return pltpu.CompilerParams(dimension_semantics=sem, vmem_limit_bytes=VMEM_LIMIT_BYTES)


def _silu(v):
    return v * jax.nn.sigmoid(v)


def _norm_mod(x, mod, g, row0, n_ctx, k):
    y = x * lax.rsqrt(jnp.mean(x * x, axis=-1, keepdims=True) + EPS) * g
    rows = row0 + lax.broadcasted_iota(jnp.int32, (x.shape[0], 1), 0)
    is_ctx = rows < n_ctx
    shift = jnp.where(is_ctx, mod[3 * k:3 * k + 1], mod[8 + 3 * k:9 + 3 * k])
    scale = jnp.where(is_ctx, mod[3 * k + 1:3 * k + 2], mod[9 + 3 * k:10 + 3 * k])
    return y * (1.0 + scale) + shift


def _res_gate(mod, row0, n_rows, n_ctx, k):
    rows = row0 + lax.broadcasted_iota(jnp.int32, (n_rows, 1), 0)
    return jnp.where(rows < n_ctx, mod[3 * k + 2:3 * k + 3], mod[10 + 3 * k:11 + 3 * k])


def _ada_kernel(c_ref, w_ref, b_ref, o_ref):
    o_ref[...] = jnp.dot(_silu(c_ref[...]), w_ref[...], preferred_element_type=F32, precision=HIGHEST) + b_ref[...]


def _ada(cvec, w, b):
    n = w.shape[1]
    tn = _tile(n, 1536, LANES)
    return pl.pallas_call(
        _ada_kernel,
        out_shape=jax.ShapeDtypeStruct((cvec.shape[0], n), F32),
        grid=(n // tn,),
        in_specs=[pl.BlockSpec(cvec.shape, lambda j: (0, 0)),
                  pl.BlockSpec((w.shape[0], tn), lambda j: (0, j)),
                  pl.BlockSpec((1, tn), lambda j: (0, j))],
        out_specs=pl.BlockSpec((cvec.shape[0], tn), lambda j: (0, j)),
        compiler_params=_cparams(("arbitrary",)),
    )(cvec, w, b.reshape(1, n))


def _inproj_kernel(x_ref, mod_ref, g_ref, w_ref, o_ref, dt_ref, h_ref, *, tm, tn, n_ctx):
    i = pl.program_id(1)
    j = pl.program_id(2)

    @pl.when(j == 0)
    def _():
        h_ref[...] = _norm_mod(x_ref[0], mod_ref[0], g_ref[...], i * tm, n_ctx, 0).astype(BF16)

    res = jnp.dot(h_ref[...], w_ref[...], preferred_element_type=F32)
    o_ref[0] = res.astype(BF16)

    @pl.when(j == COL_DT // tn)
    def _():
        dt_ref[0] = res[:, COL_DT % tn:COL_DT % tn + LANES]


def _inproj(xs, modtab, g, w, n_ctx):
    b, nt, d = xs.shape
    n = w.shape[1]
    tm = _tile(nt, 1280, 256)
    tn = _tile(n, 1024, LANES)
    assert COL_DT % tn + LANES <= tn
    return pl.pallas_call(
        functools.partial(_inproj_kernel, tm=tm, tn=tn, n_ctx=n_ctx),
        out_shape=(jax.ShapeDtypeStruct((b, nt, n), BF16), jax.ShapeDtypeStruct((b, nt, LANES), F32)),
        grid=(b, nt // tm, n // tn),
        in_specs=[pl.BlockSpec((1, tm, d), lambda bi, i, j: (bi, i, 0)),
                  pl.BlockSpec((1, 16, d), lambda bi, i, j: (bi, 0, 0)),
                  pl.BlockSpec((1, d), lambda bi, i, j: (0, 0)),
                  pl.BlockSpec((d, tn), lambda bi, i, j: (0, j))],
        out_specs=(pl.BlockSpec((1, tm, tn), lambda bi, i, j: (bi, i, j)),
                   pl.BlockSpec((1, tm, LANES), lambda bi, i, j: (bi, i, 0))),
        scratch_shapes=[pltpu.VMEM((tm, d), BF16)],
        compiler_params=_cparams(("parallel", "parallel", "arbitrary")),
    )(xs, modtab, g.reshape(1, d), w)


def _halo_flags(i, t, n_ctx, n_tiles):
    zero_prev = jnp.logical_or(i == 0, i * t == n_ctx)
    zero_next = jnp.logical_or((i + 1) * t == n_ctx, i == n_tiles - 1)
    return zero_prev, zero_next


def _convmod_kernel(cur_ref, prev_ref, next_ref, w_ref, b_ref, lg_ref, lb_ref, o_ref, buf_ref, acc_ref,
                    *, t, n_ctx, n_tiles):
    i = pl.program_id(1)
    zero_prev, zero_next = _halo_flags(i, t, n_ctx, n_tiles)

    def glu(u):
        u = u.astype(F32)
        return u[:, :CONV_DIM] * jax.nn.sigmoid(u[:, CONV_DIM:])

    halo = 16
    buf_ref[0, 0:halo, :] = jnp.where(zero_prev, 0.0, glu(prev_ref[0]))
    buf_ref[0, halo:halo + t, :] = glu(cur_ref[0])
    buf_ref[0, halo + t:2 * halo + t, :] = jnp.where(zero_next, 0.0, glu(next_ref[0]))
    n_sh = t + 2 * halo - SUBLANES
    for sh in range(1, SUBLANES):
        buf_ref[sh, 0:n_sh, :] = buf_ref[0, sh:sh + n_sh, :]
    pad = CONV_WIDTH // 2
    rc = 64
    for r in range(t // rc):
        for c in range(CONV_DIM // LANES):
            cs = slice(c * LANES, (c + 1) * LANES)
            acc = jnp.zeros((rc, LANES), F32)
            for k in range(CONV_WIDTH):
                off = halo + k - pad
                base = r * rc + off - off % SUBLANES
                acc = acc + w_ref[k:k + 1, cs] * buf_ref[off % SUBLANES, base:base + rc, cs]
            acc_ref[r * rc:(r + 1) * rc, cs] = acc + b_ref[:, cs]
    h = acc_ref[...]
    mu = jnp.mean(h, axis=-1, keepdims=True)
    var = jnp.mean(jnp.square(h - mu), axis=-1, keepdims=True)
    y = (h - mu) * lax.rsqrt(var + EPS) * lg_ref[...] + lb_ref[...]
    o_ref[0] = _silu(y)


def _convmod(u, w, bias, ln_g, ln_b, n_ctx):
    b, nt, _ = u.shape
    t = 256
    n_tiles = nt // t
    hb = t // 16
    last_hb = nt // 16 - 1
    width = 2 * CONV_DIM
    cb = COL_CONV // width
    return pl.pallas_call(
        functools.partial(_convmod_kernel, t=t, n_ctx=n_ctx, n_tiles=n_tiles),
        out_shape=jax.ShapeDtypeStruct((b, nt, CONV_DIM), F32),
        grid=(b, n_tiles),
        in_specs=[pl.BlockSpec((1, t, width), lambda bi, i: (bi, i, cb)),
                  pl.BlockSpec((1, 16, width), lambda bi, i: (bi, jnp.maximum(i * hb - 1, 0), cb)),
                  pl.BlockSpec((1, 16, width), lambda bi, i: (bi, jnp.minimum((i + 1) * hb, last_hb), cb)),
                  pl.BlockSpec((CONV_WIDTH, CONV_DIM), lambda bi, i: (0, 0)),
                  pl.BlockSpec((1, CONV_DIM), lambda bi, i: (0, 0)),
                  pl.BlockSpec((1, CONV_DIM), lambda bi, i: (0, 0)),
                  pl.BlockSpec((1, CONV_DIM), lambda bi, i: (0, 0))],
        out_specs=pl.BlockSpec((1, t, CONV_DIM), lambda bi, i: (bi, i, 0)),
        scratch_shapes=[pltpu.VMEM((SUBLANES, t + 32, CONV_DIM), F32), pltpu.VMEM((t, CONV_DIM), F32)],
        compiler_params=_cparams(("parallel", "parallel")),
    )(u, u, u, w, bias.reshape(1, -1), ln_g.reshape(1, -1), ln_b.reshape(1, -1))


def _ssmconv_kernel(cur_ref, prev_ref, next_ref, w_ref, b_ref, o_ref, buf_ref, *, t, n_ctx, n_tiles):
    i = pl.program_id(1)
    zero_prev, zero_next = _halo_flags(i, t, n_ctx, n_tiles)
    halo = 16
    buf_ref[0:halo, :] = jnp.where(zero_prev, 0.0, prev_ref[0].astype(F32))
    buf_ref[halo:halo + t, :] = cur_ref[0].astype(F32)
    buf_ref[halo + t:2 * halo + t, :] = jnp.where(zero_next, 0.0, next_ref[0].astype(F32))
    pad = SSM_CONV // 2
    rc = 64
    for r in range(t // rc):
        for c in range(SSM_XBC // LANES):
            cs = slice(c * LANES, (c + 1) * LANES)
            acc = jnp.zeros((rc, LANES), F32)
            for k in range(SSM_CONV):
                off = halo + r * rc + k - pad
                acc = acc + w_ref[k:k + 1, cs] * buf_ref[off:off + rc, cs]
            o_ref[0, r * rc:(r + 1) * rc, cs] = _silu(acc + b_ref[:, cs])


def _ssmconv(u, w, bias, n_ctx):
    b, nt, _ = u.shape
    t = 256
    n_tiles = nt // t
    hb = t // 16
    last_hb = nt // 16 - 1
    cb = COL_XBC // SSM_XBC
    return pl.pallas_call(
        functools.partial(_ssmconv_kernel, t=t, n_ctx=n_ctx, n_tiles=n_tiles),
        out_shape=jax.ShapeDtypeStruct((b, nt, SSM_XBC), F32),
        grid=(b, n_tiles),
        in_specs=[pl.BlockSpec((1, t, SSM_XBC), lambda bi, i: (bi, i, cb)),
                  pl.BlockSpec((1, 16, SSM_XBC), lambda bi, i: (bi, jnp.maximum(i * hb - 1, 0), cb)),
                  pl.BlockSpec((1, 16, SSM_XBC), lambda bi, i: (bi, jnp.minimum((i + 1) * hb, last_hb), cb)),
                  pl.BlockSpec((SSM_CONV, SSM_XBC), lambda bi, i: (0, 0)),
                  pl.BlockSpec((1, SSM_XBC), lambda bi, i: (0, 0))],
        out_specs=pl.BlockSpec((1, t, SSM_XBC), lambda bi, i: (bi, i, 0)),
        scratch_shapes=[pltpu.VMEM((t + 32, SSM_XBC), F32)],
        compiler_params=_cparams(("parallel", "parallel")),
    )(u, u, u, w, bias.reshape(1, -1))


SCAN_CHUNKS = 2


def _chunk_index(s, reverse, n_chunks, n_ctx_chunks):
    if not reverse:
        return s
    return jnp.where(s < n_ctx_chunks, n_ctx_chunks - 1 - s, n_chunks + n_ctx_chunks - 1 - s)


def _softplus(v):
    return jnp.maximum(v, 0.0) + jnp.log(1.0 + jnp.exp(-jnp.abs(v)))


def _ssd_chunk(refs, rows, reverse):
    if reverse:
        xa_ref, dt_ref, dtb_ref, alog_ref, yf_ref, z_ref, dskip_ref, ng_ref, o_ref, h_ref = refs
    else:
        xa_ref, dt_ref, dtb_ref, alog_ref, o_ref, h_ref = refs
    L = CHUNK
    xa = xa_ref[0, rows, :]
    dt = _softplus(dt_ref[0, rows, :] + dtb_ref[...])
    la = dt * (-jnp.exp(alog_ref[...]))
    r = lax.broadcasted_iota(jnp.int32, (L, L), 0)
    c = lax.broadcasted_iota(jnp.int32, (L, L), 1)
    keep = (r <= c) if reverse else (r >= c)
    cum = jnp.dot(keep.astype(F32), la, preferred_element_type=F32, precision=HIGHEST)
    cum_t = cum.T
    off = SSM_HEADS if reverse else 0
    tot_row = 0 if reverse else L - 1
    hpg = SSM_HEADS // SSM_GROUPS
    gw = hpg * SSM_HEAD_DIM
    lane_head = lax.broadcasted_iota(jnp.int32, (L, gw), 1) // SSM_HEAD_DIM
    ys = []
    for g in range(SSM_GROUPS):
        b_f = xa[:, SSM_INNER + g * SSM_STATE:SSM_INNER + (g + 1) * SSM_STATE]
        c_b = xa[:, SSM_INNER + (SSM_GROUPS + g) * SSM_STATE:SSM_INNER + (SSM_GROUPS + g + 1) * SSM_STATE].astype(BF16)
        gram = lax.dot_general(c_b, b_f.astype(BF16), (((1,), (1,)), ((), ())), preferred_element_type=F32)
        b_t = b_f.T.astype(BF16)

        def expand(mat):
            out = jnp.broadcast_to(mat[:, off + g * hpg + hpg - 1:off + g * hpg + hpg], (L, gw))
            for j in range(hpg - 2, -1, -1):
                out = jnp.where(lane_head == j, mat[:, off + g * hpg + j:off + g * hpg + j + 1], out)
            return out

        dt_l = expand(dt)
        cum_l = expand(cum)
        tot_l = cum_l[tot_row:tot_row + 1, :]
        xdt = xa[:, g * gw:(g + 1) * gw] * dt_l
        xdt_b = xdt.astype(BF16)
        ms = []
        for j in range(hpg):
            hh = off + g * hpg + j
            seg = cum[:, hh:hh + 1] - cum_t[hh:hh + 1, :]
            dec = jnp.exp(jnp.where(keep, seg, -jnp.inf))
            ms.append((gram * dec).astype(BF16))
        rr = jnp.dot(jnp.concatenate(ms, axis=0), xdt_b, preferred_element_type=F32)
        y = jnp.where(lane_head == 0, rr[0:L], 0.0)
        for j in range(1, hpg):
            y = y + jnp.where(lane_head == j, rr[j * L:(j + 1) * L], 0.0)
        hg = h_ref[g]
        y = y + jnp.dot(c_b, hg.astype(BF16), preferred_element_type=F32) * jnp.exp(cum_l)
        wm = jnp.exp(tot_l - cum_l)
        h_ref[g] = jnp.exp(tot_l) * hg + jnp.dot(b_t, (xdt * wm).astype(BF16), preferred_element_type=F32)
        ys.append(y)
    y = jnp.concatenate(ys, axis=1)
    if not reverse:
        o_ref[0, rows, :] = y
        return
    y = yf_ref[0, rows, :] + y + dskip_ref[...] * xa[:, :SSM_INNER]
    y = y * _silu(z_ref[0, rows, :].astype(F32))
    outs = []
    for g in range(SSM_GROUPS):
        yy = y[:, g * gw:(g + 1) * gw]
        outs.append(yy * lax.rsqrt(jnp.mean(yy * yy, axis=-1, keepdims=True) + EPS) * ng_ref[:, g * gw:(g + 1) * gw])
    o_ref[0, rows, :] = jnp.concatenate(outs, axis=1)


def _ssd_specs(xa, u, dt_raw, dt_bias, a_log, yf, d_skip, norm_g, cidx, blk, reverse):
    pad16 = lambda p: jnp.pad(p.reshape(1, -1), ((0, 0), (0, LANES - 2 * SSM_HEADS)))
    in_specs = [pl.BlockSpec((1, blk, SSM_XBC), lambda bi, s: (bi, cidx(s), 0)),
                pl.BlockSpec((1, blk, LANES), lambda bi, s: (bi, cidx(s), 0)),
                pl.BlockSpec((1, LANES), lambda bi, s: (0, 0)),
                pl.BlockSpec((1, LANES), lambda bi, s: (0, 0))]
    args = [xa, dt_raw, pad16(dt_bias), pad16(a_log)]
    if reverse:
        in_specs += [pl.BlockSpec((1, blk, SSM_INNER), lambda bi, s: (bi, cidx(s), 0)),
                     pl.BlockSpec((1, blk, SSM_INNER), lambda bi, s: (bi, cidx(s), COL_Z // SSM_INNER)),
                     pl.BlockSpec((1, SSM_INNER), lambda bi, s: (0, 0)),
                     pl.BlockSpec((1, SSM_INNER), lambda bi, s: (0, 0))]
        args += [yf, u, jnp.repeat(d_skip, SSM_HEAD_DIM).reshape(1, -1), norm_g.reshape(1, -1)]
    return in_specs, args


def _ret_rope(v, cos, sin):
    lo, hi = v[:, :LANES], v[:, LANES:]
    return jnp.concatenate([lo * cos - hi * sin, lo * sin + hi * cos], axis=1)


def _ret_chunk(refs, rows, reverse):
    if reverse:
        (q_ref, k_ref, v_ref, cos_ref, sin_ref, dec_ref, yf_ref, g_ref, gng_ref, gnb_ref, o_ref, h_ref) = refs
    else:
        (q_ref, k_ref, v_ref, cos_ref, sin_ref, dec_ref, decb_ref, o_ref, h_ref) = refs
    L = CHUNK
    qk = RET_HEADS * RET_QK_DIM
    cos, sin = cos_ref[rows, :], sin_ref[rows, :]
    q = _ret_rope(q_ref[0, rows, :].astype(F32), cos, sin)
    k = _ret_rope(k_ref[0, rows, :].astype(F32), cos, sin) * (RET_QK_DIM ** -0.5)
    v = v_ref[0, rows, :].astype(F32)
    q_b = q.astype(BF16)
    k_t = k.T.astype(BF16)
    lg = -jnp.exp(dec_ref[...])
    t = lax.broadcasted_iota(jnp.int32, (L, 1), 0).astype(F32)
    if reverse:
        e_in = jnp.exp(lg * (L - t))
        w_st = jnp.exp(lg * t)
    else:
        e_in = jnp.exp(lg * (t + 1.0))
        w_st = jnp.exp(lg * (L - 1.0 - t))
    hs = h_ref[...]
    y = jnp.dot(q_b, hs.astype(BF16), preferred_element_type=F32) * e_in
    upd = jnp.dot(k_t, (v * w_st).astype(BF16), preferred_element_type=F32)
    row_head = (lax.broadcasted_iota(jnp.int32, (qk, RET_INNER), 0) % LANES) // (RET_QK_DIM // 2)
    lane_head = lax.broadcasted_iota(jnp.int32, (qk, RET_INNER), 1) // RET_V_DIM
    h_ref[...] = jnp.exp(lg * float(L)) * hs + jnp.where(row_head == lane_head, upd, 0.0)

    if not reverse:
        lgb = -jnp.exp(decb_ref[...])
        q_head = (lax.broadcasted_iota(jnp.int32, (L, qk), 1) % LANES) // (RET_QK_DIM // 2)
        qs = jnp.concatenate([jnp.where(q_head == h, q, 0.0) for h in range(RET_HEADS)], axis=0).astype(BF16)
        sc = jnp.dot(qs, k_t, preferred_element_type=F32)
        dl = (lax.broadcasted_iota(jnp.int32, (L, L), 0) - lax.broadcasted_iota(jnp.int32, (L, L), 1)).astype(F32)
        parts = []
        for h in range(RET_HEADS):
            vs = slice(h * RET_V_DIM, (h + 1) * RET_V_DIM)
            dcomb = jnp.where(dl > 0, jnp.exp(lg[:, vs] * jnp.maximum(dl, 0.0)),
                              jnp.where(dl < 0, jnp.exp(lgb[:, vs] * jnp.maximum(-dl, 0.0)), 2.0))
            p = (sc[h * L:(h + 1) * L] * dcomb).astype(BF16)
            parts.append(jnp.dot(p, v[:, vs].astype(BF16), preferred_element_type=F32))
        o_ref[0, rows, :] = y + jnp.concatenate(parts, axis=1)
        return
    y = yf_ref[0, rows, :] + y
    outs = []
    for h in range(RET_HEADS):
        vs = slice(h * RET_V_DIM, (h + 1) * RET_V_DIM)
        yy = y[:, vs]
        mu = jnp.mean(yy, axis=-1, keepdims=True)
        var = jnp.mean(jnp.square(yy - mu), axis=-1, keepdims=True)
        outs.append((yy - mu) * lax.rsqrt(var + EPS) * gng_ref[:, vs] + gnb_ref[:, vs])
    o_ref[0, rows, :] = _silu(g_ref[0, rows, :].astype(F32)) * jnp.concatenate(outs, axis=1)


def _ret_specs(u, cos4, sin4, ret_decay, yf, gn_g, gn_b, cidx, blk, reverse):
    qk = RET_HEADS * RET_QK_DIM
    lane_dec = lambda d: jnp.repeat(d, RET_V_DIM).reshape(1, -1)
    in_specs = [pl.BlockSpec((1, blk, qk), lambda bi, s: (bi, cidx(s), COL_RQ // qk)),
                pl.BlockSpec((1, blk, qk), lambda bi, s: (bi, cidx(s), COL_RK // qk)),
                pl.BlockSpec((1, blk, RET_INNER), lambda bi, s: (bi, cidx(s), COL_RV // RET_INNER)),
                pl.BlockSpec((blk, LANES), lambda bi, s: (cidx(s), 0)),
                pl.BlockSpec((blk, LANES), lambda bi, s: (cidx(s), 0)),
                pl.BlockSpec((1, RET_INNER), lambda bi, s: (0, 0))]
    args = [u, u, u, cos4, sin4, lane_dec(ret_decay[1] if reverse else ret_decay[0])]
    if reverse:
        in_specs += [pl.BlockSpec((1, blk, RET_INNER), lambda bi, s: (bi, cidx(s), 0)),
                     pl.BlockSpec((1, blk, RET_INNER), lambda bi, s: (bi, cidx(s), COL_RG // RET_INNER)),
                     pl.BlockSpec((1, RET_INNER), lambda bi, s: (0, 0)),
                     pl.BlockSpec((1, RET_INNER), lambda bi, s: (0, 0))]
        args += [yf, u, gn_g.reshape(1, -1), gn_b.reshape(1, -1)]
    else:
        in_specs += [pl.BlockSpec((1, RET_INNER), lambda bi, s: (0, 0))]
        args += [lane_dec(ret_decay[1])]
    return in_specs, args


def _scan_kernel(*refs, n_ssd_in, reverse):
    n_in = len(refs) - 4
    ssd_o, ret_o, ssd_h, ret_h = refs[n_in:]
    ssd_refs = refs[:n_ssd_in] + (ssd_o, ssd_h)
    ret_refs = refs[n_ssd_in:n_in] + (ret_o, ret_h)

    @pl.when(pl.program_id(1) == 0)
    def _():
        ssd_h[...] = jnp.zeros_like(ssd_h)
        ret_h[...] = jnp.zeros_like(ret_h)

    order = range(SCAN_CHUNKS - 1, -1, -1) if reverse else range(SCAN_CHUNKS)
    for ci in order:
        rows = slice(ci * CHUNK, (ci + 1) * CHUNK)
        _ssd_chunk(ssd_refs, rows, reverse)
        _ret_chunk(ret_refs, rows, reverse)


def _scan(ssd_in, ret_in, n_ctx, reverse):
    b, nt, _ = ssd_in[0].shape
    blk = SCAN_CHUNKS * CHUNK
    cidx = functools.partial(_chunk_index, reverse=reverse, n_chunks=nt // blk, n_ctx_chunks=n_ctx // blk)
    ssd_specs, ssd_args = _ssd_specs(*ssd_in, cidx, blk, reverse)
    ret_specs, ret_args = _ret_specs(*ret_in, cidx, blk, reverse)
    out_spec = lambda w: pl.BlockSpec((1, blk, w), lambda bi, s: (bi, cidx(s), 0))
    return pl.pallas_call(
        functools.partial(_scan_kernel, n_ssd_in=len(ssd_args), reverse=reverse),
        out_shape=(jax.ShapeDtypeStruct((b, nt, SSM_INNER), F32), jax.ShapeDtypeStruct((b, nt, RET_INNER), F32)),
        grid=(b, nt // blk),
        in_specs=ssd_specs + ret_specs,
        out_specs=(out_spec(SSM_INNER), out_spec(RET_INNER)),
        scratch_shapes=[pltpu.VMEM((SSM_GROUPS, SSM_STATE, SSM_INNER // SSM_GROUPS), F32),
                        pltpu.VMEM((RET_HEADS * RET_QK_DIM, RET_INNER), F32)],
        compiler_params=_cparams(("parallel", "arbitrary")),
    )(*ssd_args, *ret_args)


def _mla_rope(v, c, s1, s2):
    return v * c + pltpu.roll(v, 16, 1) * s1 + pltpu.roll(v, LANES - 16, 1) * s2


def _rms(v, g):
    return v * lax.rsqrt(jnp.mean(v * v, axis=-1, keepdims=True) + EPS) * g


def _mlaprep_kernel(cq_ref, ckv_ref, kr_ref, qg_ref, kvg_ref, wq_ref, wk_ref, wv_ref, c_ref, s1_ref, s2_ref,
                    qt_out, k_out, vt_out):
    c, s1, s2 = c_ref[...], s1_ref[...], s2_ref[...]
    scale = (MLA_NOPE + MLA_ROPE) ** -0.5 * math.log2(math.e)
    q = jnp.dot(_rms(cq_ref[0].astype(F32), qg_ref[...]).astype(BF16), wq_ref[...], preferred_element_type=F32)
    ckv = _rms(ckv_ref[0].astype(F32), kvg_ref[...]).astype(BF16)
    kn = jnp.dot(ckv, wk_ref[...], preferred_element_type=F32)
    vt_out[0] = jnp.dot(ckv, wv_ref[...], preferred_element_type=F32).T.astype(BF16)
    krr = _mla_rope(kr_ref[0].astype(F32), c, s1, s2)
    for h in range(MLA_HEADS):
        hs = slice(h * LANES, (h + 1) * LANES)
        qt_out[0, hs, :] = (_mla_rope(q[:, hs], c, s1, s2) * scale).T.astype(BF16)
        k_out[0, :, hs] = (kn[:, hs] + krr).astype(BF16)


def _mlaprep(u, q_g, kv_g, wq, wk, wv, ctab, s1tab, s2tab):
    b, nt, _ = u.shape
    tm = _tile(nt, 1280, 256)
    hw = MLA_HEADS * LANES
    const = lambda shape: pl.BlockSpec(shape, lambda bi, i: (0, 0))
    return pl.pallas_call(
        _mlaprep_kernel,
        out_shape=(jax.ShapeDtypeStruct((b, hw, nt), BF16), jax.ShapeDtypeStruct((b, nt, hw), BF16),
                   jax.ShapeDtypeStruct((b, MLA_HEADS * MLA_V, nt), BF16)),
        grid=(b, nt // tm),
        in_specs=[pl.BlockSpec((1, tm, MLA_Q_RANK), lambda bi, i: (bi, i, COL_CQ // MLA_Q_RANK)),
                  pl.BlockSpec((1, tm, MLA_KV_RANK), lambda bi, i: (bi, i, COL_CKV // MLA_KV_RANK)),
                  pl.BlockSpec((1, tm, LANES), lambda bi, i: (bi, i, COL_KR // LANES)),
                  const((1, MLA_Q_RANK)), const((1, MLA_KV_RANK)),
                  const((MLA_Q_RANK, hw)), const((MLA_KV_RANK, hw)), const((MLA_KV_RANK, MLA_HEADS * MLA_V)),
                  pl.BlockSpec((tm, LANES), lambda bi, i: (i, 0)),
                  pl.BlockSpec((tm, LANES), lambda bi, i: (i, 0)),
                  pl.BlockSpec((tm, LANES), lambda bi, i: (i, 0))],
        out_specs=(pl.BlockSpec((1, hw, tm), lambda bi, i: (bi, 0, i)),
                   pl.BlockSpec((1, tm, hw), lambda bi, i: (bi, i, 0)),
                   pl.BlockSpec((1, MLA_HEADS * MLA_V, tm), lambda bi, i: (bi, 0, i))),
        compiler_params=_cparams(("parallel", "parallel")),
    )(u, u, u, q_g.reshape(1, -1), kv_g.reshape(1, -1), wq, wk, wv, ctab, s1tab, s2tab)


ACC_ROWS = MLA_V + 16


def _flash_kernel(k_ref, qt_ref, vt_ref, o_ref, m_ref, acc_ref, *, tq, tk, n_ctx, nk):
    i = pl.program_id(2)
    j = pl.program_id(3)

    @pl.when(j == 0)
    def _():
        m_ref[...] = jnp.full_like(m_ref, -jnp.inf)
        acc_ref[...] = jnp.zeros_like(acc_ref)

    def step(masked):
        ones = jnp.ones((ACC_ROWS - MLA_V, tk), BF16)
        for hh in range(2):
            hs = slice(hh * LANES, (hh + 1) * LANES)
            s = jnp.dot(k_ref[0, :, hs], qt_ref[0, hs, :], preferred_element_type=F32)
            if masked:
                keys = j * tk + lax.broadcasted_iota(jnp.int32, (tk, n_ctx), 0)
                s_ctx = jnp.where(keys >= n_ctx, -1e30, s[:, :n_ctx])
                s = s_ctx if tq == n_ctx else jnp.concatenate([s_ctx, s[:, n_ctx:]], axis=1)
            m_prev = m_ref[hh]
            m_new = jnp.maximum(m_prev, jnp.max(s, axis=0, keepdims=True))
            alpha = jnp.exp2(m_prev - m_new)
            p = jnp.exp2(s - m_new).astype(BF16)
            vs = slice(hh * MLA_V, (hh + 1) * MLA_V)
            lhs = jnp.concatenate([vt_ref[0, vs, :], ones], axis=0)
            acc_ref[hh] = alpha * acc_ref[hh] + jnp.dot(lhs, p, preferred_element_type=F32)
            m_ref[hh] = m_new

    @pl.when(i * tq < n_ctx)
    def _():
        step(True)

    @pl.when(i * tq >= n_ctx)
    def _():
        step(False)

    @pl.when(j == nk - 1)
    def _():
        outs = []
        for hh in range(2):
            a = acc_ref[hh]
            outs.append(a[:MLA_V] * (1.0 / a[MLA_V:MLA_V + 1]))
        o_ref[0] = jnp.concatenate(outs, axis=0).T


def _flash(k, qt, vt, n_ctx, tq_target=3328, tk_target=1280):
    b, nt, _ = k.shape
    tq = _tile(nt, tq_target, LANES)
    tk = _tile(nt, tk_target, LANES)
    assert tq >= n_ctx and n_ctx % LANES == 0
    nk = nt // tk
    pairs = MLA_HEADS // 2
    return pl.pallas_call(
        functools.partial(_flash_kernel, tq=tq, tk=tk, n_ctx=n_ctx, nk=nk),
        out_shape=jax.ShapeDtypeStruct((b, nt, MLA_HEADS * MLA_V), F32),
        grid=(b, pairs, nt // tq, nk),
        in_specs=[pl.BlockSpec((1, tk, 2 * LANES), lambda bi, p, i, j: (bi, j, p)),
                  pl.BlockSpec((1, 2 * LANES, tq), lambda bi, p, i, j: (bi, p, i)),
                  pl.BlockSpec((1, 2 * MLA_V, tk), lambda bi, p, i, j: (bi, p, j))],
        out_specs=pl.BlockSpec((1, tq, 2 * MLA_V), lambda bi, p, i, j: (bi, i, p)),
        scratch_shapes=[pltpu.VMEM((2, 1, tq), F32), pltpu.VMEM((2, ACC_ROWS, tq), F32)],
        compiler_params=_cparams(("parallel", "parallel", "parallel", "arbitrary")),
    )(k, qt, vt)


def _merge_kernel(xs_ref, gl_ref, b0_ref, b1_ref, b2_ref, b3_ref, wb_ref, wo_ref, mod_ref, o_ref, *, tm, n_ctx):
    i = pl.program_id(1)
    merged = None
    for n, br in enumerate((b0_ref, b1_ref, b2_ref, b3_ref)):
        proj = jnp.dot(br[0].astype(BF16), wb_ref[n], preferred_element_type=F32)
        term = jax.nn.sigmoid(gl_ref[0, :, n * D_MODEL:(n + 1) * D_MODEL].astype(F32)) * proj
        merged = term if merged is None else merged + term
    out = jnp.dot(merged.astype(BF16), wo_ref[...], preferred_element_type=F32)
    o_ref[0] = xs_ref[0] + _res_gate(mod_ref[0], i * tm, tm, n_ctx, 0) * out


def _merge(xs, u, branches, wb, wo, modtab, n_ctx):
    b, nt, d = xs.shape
    tm = 256
    row = lambda w: pl.BlockSpec((1, tm, w), lambda bi, i: (bi, i, 0))
    return pl.pallas_call(
        functools.partial(_merge_kernel, tm=tm, n_ctx=n_ctx),
        out_shape=jax.ShapeDtypeStruct((b, nt, d), F32),
        grid=(b, nt // tm),
        in_specs=[row(d), row(N_BRANCH * d), row(BRANCH_DIM), row(BRANCH_DIM), row(BRANCH_DIM), row(BRANCH_DIM),
                  pl.BlockSpec((N_BRANCH, BRANCH_DIM, d), lambda bi, i: (0, 0, 0)),
                  pl.BlockSpec((d, d), lambda bi, i: (0, 0)),
                  pl.BlockSpec((1, 16, d), lambda bi, i: (bi, 0, 0))],
        out_specs=row(d),
        compiler_params=_cparams(("parallel", "parallel")),
    )(xs, u, *branches, wb, wo, modtab)


def _ffn_kernel(xs_ref, mod_ref, g_ref, wa_ref, wg_ref, wo_ref, o_ref, h_ref, acc_ref, *, tm, n_ctx, nj):
    i = pl.program_id(1)
    j = pl.program_id(2)

    @pl.when(j == 0)
    def _():
        h_ref[...] = _norm_mod(xs_ref[0], mod_ref[0], g_ref[...], i * tm, n_ctx, 1).astype(BF16)

    h = h_ref[...]
    a = jnp.dot(h, wa_ref[...], preferred_element_type=F32)
    gate = jnp.dot(h, wg_ref[...], preferred_element_type=F32)
    part = jnp.dot((_silu(gate) * a).astype(BF16), wo_ref[...], preferred_element_type=F32)

    @pl.when(j == 0)
    def _():
        acc_ref[...] = part

    @pl.when(j > 0)
    def _():
        acc_ref[...] += part

    @pl.when(j == nj - 1)
    def _():
        o_ref[0] = xs_ref[0] + _res_gate(mod_ref[0], i * tm, tm, n_ctx, 1) * acc_ref[...]


def _ffn(xs, modtab, g, w_in, w_out, n_ctx):
    b, nt, d = xs.shape
    f = w_out.shape[0]
    tm = _tile(nt, 640, 128)
    tf = _tile(f, 1408, LANES)
    nj = f // tf
    return pl.pallas_call(
        functools.partial(_ffn_kernel, tm=tm, n_ctx=n_ctx, nj=nj),
        out_shape=jax.ShapeDtypeStruct((b, nt, d), F32),
        grid=(b, nt // tm, nj),
        in_specs=[pl.BlockSpec((1, tm, d), lambda bi, i, j: (bi, i, 0)),
                  pl.BlockSpec((1, 16, d), lambda bi, i, j: (bi, 0, 0)),
                  pl.BlockSpec((1, d), lambda bi, i, j: (0, 0)),
                  pl.BlockSpec((d, tf), lambda bi, i, j: (0, j)),
                  pl.BlockSpec((d, tf), lambda bi, i, j: (0, nj + j)),
                  pl.BlockSpec((tf, d), lambda bi, i, j: (j, 0))],
        out_specs=pl.BlockSpec((1, tm, d), lambda bi, i, j: (bi, i, 0)),
        scratch_shapes=[pltpu.VMEM((tm, d), BF16), pltpu.VMEM((tm, d), F32)],
        compiler_params=_cparams(("parallel", "parallel", "arbitrary")),
    )(xs, modtab, g.reshape(1, d), w_in, w_in, w_out)


def _final_kernel(x_ref, g_ref, o_ref):
    o_ref[0] = _rms(x_ref[0], g_ref[...])


def _final_norm(xs, g, n_ctx):
    b, nt, d = xs.shape
    t = 256
    skip = n_ctx // t
    return pl.pallas_call(
        _final_kernel,
        out_shape=jax.ShapeDtypeStruct((b, nt - n_ctx, d), F32),
        grid=(b, (nt - n_ctx) // t),
        in_specs=[pl.BlockSpec((1, t, d), lambda bi, i: (bi, i + skip, 0)),
                  pl.BlockSpec((1, d), lambda bi, i: (0, 0))],
        out_specs=pl.BlockSpec((1, t, d), lambda bi, i: (bi, i, 0)),
        compiler_params=_cparams(("parallel", "parallel")),
    )(xs, g.reshape(1, d))


def _prep_w_in(w):
    d = w.shape[0]
    o_conv, o_z, o_xbc, o_dt = 0, 1024, 1536, 2560
    o_rq, o_rk, o_rv, o_rg = 2576, 2832, 3088, 3600
    o_cq, o_ckv, o_kr, o_gl = 4112, 4496, 4752, 4784
    seg = lambda o, n: w[:, o:o + n]

    def halves(o):
        s = seg(o, RET_HEADS * RET_QK_DIM).reshape(d, RET_HEADS, 2, RET_QK_DIM // 2)
        return jnp.transpose(s, (0, 2, 1, 3)).reshape(d, RET_HEADS * RET_QK_DIM)

    zeros = lambda n: jnp.zeros((d, n), w.dtype)
    parts = [seg(o_gl, N_BRANCH * D_MODEL), seg(o_conv, 2 * CONV_DIM), seg(o_xbc, SSM_XBC), seg(o_z, SSM_INNER),
             seg(o_rv, RET_INNER), seg(o_rg, RET_INNER), halves(o_rq), halves(o_rk),
             seg(o_ckv, MLA_KV_RANK), seg(o_cq, MLA_Q_RANK),
             seg(o_dt, 2 * SSM_HEADS), zeros(LANES - 2 * SSM_HEADS),
             zeros(MLA_NOPE), seg(o_kr, MLA_ROPE), zeros(LANES - MLA_NOPE - MLA_ROPE),
             zeros(LANES)]
    out = jnp.concatenate(parts, axis=1).astype(BF16)
    assert out.shape[1] == N_IN_PAD
    return out


def _prep_mla_w(w_uq, w_ukv):
    rq, rkv = w_uq.shape[0], w_ukv.shape[0]
    hd = MLA_NOPE + MLA_ROPE
    wq = jnp.pad(w_uq.reshape(rq, MLA_HEADS, hd), ((0, 0), (0, 0), (0, LANES - hd))).reshape(rq, MLA_HEADS * LANES)
    kv = w_ukv.reshape(rkv, MLA_HEADS, MLA_NOPE + MLA_V)
    wk = jnp.pad(kv[:, :, :MLA_NOPE], ((0, 0), (0, 0), (0, LANES - MLA_NOPE))).reshape(rkv, MLA_HEADS * LANES)
    wv = kv[:, :, MLA_NOPE:].reshape(rkv, MLA_HEADS * MLA_V)
    return wq.astype(BF16), wk.astype(BF16), wv.astype(BF16)


def _axial_angles(n, rot_dim):
    rows = n // GRID_W
    row = jnp.repeat(jnp.arange(rows, dtype=F32), GRID_W)
    col = jnp.tile(jnp.arange(GRID_W, dtype=F32), rows)
    nf = rot_dim // 4
    inv = ROPE_BASE ** (-jnp.arange(nf, dtype=F32) / nf)
    return jnp.concatenate([row[:, None] * inv, col[:, None] * inv], axis=-1)


def _rope_tables(n_lat, n_ctx):
    ang = _axial_angles(n_lat, RET_QK_DIM)
    cos4 = jnp.tile(jnp.cos(ang), (1, RET_HEADS))
    sin4 = jnp.tile(jnp.sin(ang), (1, RET_HEADS))
    cos4 = jnp.concatenate([jnp.ones((n_ctx, LANES), F32), cos4], axis=0)
    sin4 = jnp.concatenate([jnp.zeros((n_ctx, LANES), F32), sin4], axis=0)
    ang = _axial_angles(n_lat, MLA_ROPE)
    c, s = jnp.cos(ang), jnp.sin(ang)
    half = MLA_ROPE // 2
    one = jnp.ones((n_lat, MLA_NOPE), F32)
    zero = lambda w: jnp.zeros((n_lat, w), F32)
    tail = LANES - MLA_NOPE - MLA_ROPE
    ctab = jnp.concatenate([one, c, c, jnp.ones((n_lat, tail), F32)], axis=1)
    s1 = jnp.concatenate([zero(MLA_NOPE + half), s, zero(tail)], axis=1)
    s2 = jnp.concatenate([zero(MLA_NOPE), -s, zero(half + tail)], axis=1)
    ctab = jnp.concatenate([jnp.ones((n_ctx, LANES), F32), ctab], axis=0)
    s1 = jnp.concatenate([jnp.zeros((n_ctx, LANES), F32), s1], axis=0)
    s2 = jnp.concatenate([jnp.zeros((n_ctx, LANES), F32), s2], axis=0)
    return cos4, sin4, ctab, s1, s2


def kernel(x, c, ctx, c_ctx, w_ada, b_ada, norm1_g, norm2_g, w_in, conv_w, conv_b, conv_ln_g, conv_ln_b,
           ssm_conv_w, ssm_conv_b, ssm_dt_bias, ssm_a_log, ssm_d, ssm_norm_g, ret_decay, ret_gn_g, ret_gn_b,
           mla_q_norm_g, mla_kv_norm_g, mla_w_uq, mla_w_ukv, w_branch, w_out, w_ffn_in, w_ffn_out, final_norm_g):
    batch, n_lat, d = x.shape
    n_ctx = ctx.shape[1]
    depth = w_in.shape[0]
    assert d == D_MODEL and n_ctx % 256 == 0 and n_lat % 256 == 0 and batch + 1 <= 8
    assert n_ctx % (SCAN_CHUNKS * CHUNK) == 0
    cos4, sin4, ctab, s1tab, s2tab = _rope_tables(n_lat, n_ctx)
    xs = jnp.concatenate([ctx, x], axis=1)
    cvec = jnp.zeros((8, d), F32).at[:batch].set(c).at[batch].set(c_ctx)
    for i in range(depth):
        mod = _ada(cvec, w_ada[i], b_ada[i]).reshape(8, 6, d)
        modtab = jnp.zeros((batch, 16, d), F32)
        modtab = modtab.at[:, 0:6].set(jnp.broadcast_to(mod[batch], (batch, 6, d))).at[:, 8:14].set(mod[:batch])
        u, dt_raw = _inproj(xs, modtab, norm1_g[i], _prep_w_in(w_in[i]), n_ctx)
        conv_y = _convmod(u, conv_w[i], conv_b[i], conv_ln_g[i], conv_ln_b[i], n_ctx)
        xa = _ssmconv(u, ssm_conv_w[i], ssm_conv_b[i], n_ctx)
        ssm_f, ret_f = _scan((xa, u, dt_raw, ssm_dt_bias[i], ssm_a_log[i], None, None, None),
                             (u, cos4, sin4, ret_decay[i], None, None, None), n_ctx, False)
        ssm_y, ret_y = _scan((xa, u, dt_raw, ssm_dt_bias[i], ssm_a_log[i], ssm_f, ssm_d[i], ssm_norm_g[i]),
                             (u, cos4, sin4, ret_decay[i], ret_f, ret_gn_g[i], ret_gn_b[i]), n_ctx, True)
        wq, wk, wv = _prep_mla_w(mla_w_uq[i], mla_w_ukv[i])
        qt, k, vt = _mlaprep(u, mla_q_norm_g[i], mla_kv_norm_g[i], wq, wk, wv, ctab, s1tab, s2tab)
        att = _flash(k, qt, vt, n_ctx)
        xs = _merge(xs, u, (conv_y, ssm_y, ret_y, att), w_branch[i].astype(BF16), w_out[i].astype(BF16),
                    modtab, n_ctx)
        xs = _ffn(xs, modtab, norm2_g[i], w_ffn_in[i].astype(BF16), w_ffn_out[i].astype(BF16), n_ctx)
    return _final_norm(xs, final_norm_g, n_ctx)
```

```python
import functools
import math

import jax
import jax.numpy as jnp
from jax import lax
from jax.experimental import pallas as pl
from jax.experimental.pallas import tpu as pltpu

F32 = jnp.float32
BF16 = jnp.bfloat16
HIGHEST = lax.Precision.HIGHEST

D_MODEL = 1024
GRID_W = 64
CHUNK = 128
ROPE_BASE = 10000.0
EPS = 1e-6
BRANCH_DIM = D_MODEL // 2
N_BRANCH = 4
CONV_DIM = BRANCH_DIM
CONV_WIDTH = 31
SSM_INNER = BRANCH_DIM
SSM_HEAD_DIM = 64
SSM_HEADS = SSM_INNER // SSM_HEAD_DIM
SSM_GROUPS = 2
SSM_STATE = 128
SSM_CONV = 5
SSM_XBC = SSM_INNER + 2 * SSM_GROUPS * SSM_STATE
RET_HEADS = 4
RET_QK_DIM = 64
RET_INNER = BRANCH_DIM
RET_V_DIM = RET_INNER // RET_HEADS
MLA_HEADS = 8
MLA_NOPE = 64
MLA_ROPE = 32
MLA_V = BRANCH_DIM // MLA_HEADS
MLA_Q_RANK = 384
MLA_KV_RANK = 256
FFN_DIM = ((8 * D_MODEL // 3 + 255) // 256) * 256

LANES = 128
SUBLANES = 8
VMEM_LIMIT_BYTES = 56 * 1024 * 1024

COL_GL = 0
COL_CONV = 4096
COL_XBC = 5120
COL_Z = 6144
COL_RV = 6656
COL_RG = 7168
COL_RQ = 7680
COL_RK = 7936
COL_CKV = 8192
COL_CQ = 8448
COL_DT = 8832
COL_KR = 8960
N_IN_PAD = 9216


def _tile(n, target, mult):
    best = None
    for t in range(mult, min(n, target) + 1, mult):
        if n % t == 0:
            best = t
    assert best is not None, (n, target, mult)
    return best


def _cparams(sem):
    return pltpu.CompilerParams(dimension_semantics=sem, vmem_limit_bytes=VMEM_LIMIT_BYTES)


def _silu(v):
    return v * jax.nn.sigmoid(v)


def _norm_mod(x, mod, g, row0, n_ctx, k):
    y = x * lax.rsqrt(jnp.mean(x * x, axis=-1, keepdims=True) + EPS) * g
    rows = row0 + lax.broadcasted_iota(jnp.int32, (x.shape[0], 1), 0)
    is_ctx = rows < n_ctx
    shift = jnp.where(is_ctx, mod[3 * k:3 * k + 1], mod[8 + 3 * k:9 + 3 * k])
    scale = jnp.where(is_ctx, mod[3 * k + 1:3 * k + 2], mod[9 + 3 * k:10 + 3 * k])
    return y * (1.0 + scale) + shift


def _res_gate(mod, row0, n_rows, n_ctx, k):
    rows = row0 + lax.broadcasted_iota(jnp.int32, (n_rows, 1), 0)
    return jnp.where(rows < n_ctx, mod[3 * k + 2:3 * k + 3], mod[10 + 3 * k:11 + 3 * k])


def _ada_kernel(c_ref, w_ref, b_ref, o_ref):
    o_ref[...] = jnp.dot(_silu(c_ref[...]), w_ref[...], preferred_element_type=F32, precision=HIGHEST) + b_ref[...]


def _ada(cvec, w, b):
    n = w.shape[1]
    tn = _tile(n, 1536, LANES)
    return pl.pallas_call(
        _ada_kernel,
        out_shape=jax.ShapeDtypeStruct((cvec.shape[0], n), F32),
        grid=(n // tn,),
        in_specs=[pl.BlockSpec(cvec.shape, lambda j: (0, 0)),
                  pl.BlockSpec((w.shape[0], tn), lambda j: (0, j)),
                  pl.BlockSpec((1, tn), lambda j: (0, j))],
        out_specs=pl.BlockSpec((cvec.shape[0], tn), lambda j: (0, j)),
        compiler_params=_cparams(("arbitrary",)),
    )(cvec, w, b.reshape(1, n))


def _inproj_kernel(x_ref, mod_ref, g_ref, w_ref, o_ref, dt_ref, h_ref, *, tm, tn, n_ctx):
    i = pl.program_id(1)
    j = pl.program_id(2)

    @pl.when(j == 0)
    def _():
        h_ref[...] = _norm_mod(x_ref[0], mod_ref[0], g_ref[...], i * tm, n_ctx, 0).astype(BF16)

    res = jnp.dot(h_ref[...], w_ref[...], preferred_element_type=F32)
    o_ref[0] = res.astype(BF16)

    @pl.when(j == COL_DT // tn)
    def _():
        dt_ref[0] = res[:, COL_DT % tn:COL_DT % tn + LANES]


def _inproj(xs, modtab, g, w, n_ctx):
    b, nt, d = xs.shape
    n = w.shape[1]
    tm = _tile(nt, 1664, LANES)
    tn = _tile(n, 1024, LANES)
    assert COL_DT % tn + LANES <= tn
    return pl.pallas_call(
        functools.partial(_inproj_kernel, tm=tm, tn=tn, n_ctx=n_ctx),
        out_shape=(jax.ShapeDtypeStruct((b, nt, n), BF16), jax.ShapeDtypeStruct((b, nt, LANES), F32)),
        grid=(b, nt // tm, n // tn),
        in_specs=[pl.BlockSpec((1, tm, d), lambda bi, i, j: (bi, i, 0)),
                  pl.BlockSpec((1, 16, d), lambda bi, i, j: (bi, 0, 0)),
                  pl.BlockSpec((1, d), lambda bi, i, j: (0, 0)),
                  pl.BlockSpec((d, tn), lambda bi, i, j: (0, j))],
        out_specs=(pl.BlockSpec((1, tm, tn), lambda bi, i, j: (bi, i, j)),
                   pl.BlockSpec((1, tm, LANES), lambda bi, i, j: (bi, i, 0))),
        scratch_shapes=[pltpu.VMEM((tm, d), BF16)],
        compiler_params=_cparams(("parallel", "parallel", "arbitrary")),
    )(xs, modtab, g.reshape(1, d), w)


def _halo_flags(i, t, n_ctx, n_tiles):
    zero_prev = jnp.logical_or(i == 0, i * t == n_ctx)
    zero_next = jnp.logical_or((i + 1) * t == n_ctx, i == n_tiles - 1)
    return zero_prev, zero_next


def _convmod_kernel(cur_ref, prev_ref, next_ref, w_ref, b_ref, lg_ref, lb_ref, o_ref, buf_ref,
                    *, t, n_ctx, n_tiles):
    i = pl.program_id(1)
    zero_prev, zero_next = _halo_flags(i, t, n_ctx, n_tiles)

    def glu(u):
        u = u.astype(F32)
        return u[:, :CONV_DIM] * jax.nn.sigmoid(u[:, CONV_DIM:])

    halo = 16
    buf_ref[0, 0:halo, :] = jnp.where(zero_prev, 0.0, glu(prev_ref[0]))
    buf_ref[0, halo:halo + t, :] = glu(cur_ref[0])
    buf_ref[0, halo + t:2 * halo + t, :] = jnp.where(zero_next, 0.0, glu(next_ref[0]))
    n_sh = t + 2 * halo - SUBLANES
    for sh in range(1, SUBLANES):
        buf_ref[sh, 0:n_sh, :] = buf_ref[0, sh:sh + n_sh, :]
    pad = CONV_WIDTH // 2
    h = jnp.zeros((t, CONV_DIM), F32)
    for k in range(CONV_WIDTH):
        off = halo + k - pad
        base = off - off % SUBLANES
        h = h + w_ref[k:k + 1, :] * buf_ref[off % SUBLANES, base:base + t, :]
    h = h + b_ref[...]
    mu = jnp.mean(h, axis=-1, keepdims=True)
    var = jnp.mean(jnp.square(h - mu), axis=-1, keepdims=True)
    y = (h - mu) * lax.rsqrt(var + EPS) * lg_ref[...] + lb_ref[...]
    o_ref[0] = _silu(y).astype(o_ref.dtype)


def _convmod(u, w, bias, ln_g, ln_b, n_ctx):
    b, nt, _ = u.shape
    t = 256
    n_tiles = nt // t
    hb = t // 16
    last_hb = nt // 16 - 1
    width = 2 * CONV_DIM
    cb = COL_CONV // width
    return pl.pallas_call(
        functools.partial(_convmod_kernel, t=t, n_ctx=n_ctx, n_tiles=n_tiles),
        out_shape=jax.ShapeDtypeStruct((b, nt, CONV_DIM), BF16),
        grid=(b, n_tiles),
        in_specs=[pl.BlockSpec((1, t, width), lambda bi, i: (bi, i, cb)),
                  pl.BlockSpec((1, 16, width), lambda bi, i: (bi, jnp.maximum(i * hb - 1, 0), cb)),
                  pl.BlockSpec((1, 16, width), lambda bi, i: (bi, jnp.minimum((i + 1) * hb, last_hb), cb)),
                  pl.BlockSpec((CONV_WIDTH, CONV_DIM), lambda bi, i: (0, 0)),
                  pl.BlockSpec((1, CONV_DIM), lambda bi, i: (0, 0)),
                  pl.BlockSpec((1, CONV_DIM), lambda bi, i: (0, 0)),
                  pl.BlockSpec((1, CONV_DIM), lambda bi, i: (0, 0))],
        out_specs=pl.BlockSpec((1, t, CONV_DIM), lambda bi, i: (bi, i, 0)),
        scratch_shapes=[pltpu.VMEM((SUBLANES, t + 32, CONV_DIM), F32)],
        compiler_params=_cparams(("parallel", "parallel")),
    )(u, u, u, w, bias.reshape(1, -1), ln_g.reshape(1, -1), ln_b.reshape(1, -1))


def _ssmconv_kernel(cur_ref, prev_ref, next_ref, w_ref, b_ref, o_ref, buf_ref, *, t, n_ctx, n_tiles):
    i = pl.program_id(1)
    zero_prev, zero_next = _halo_flags(i, t, n_ctx, n_tiles)
    halo = 16
    buf_ref[0:halo, :] = jnp.where(zero_prev, 0.0, prev_ref[0].astype(F32))
    buf_ref[halo:halo + t, :] = cur_ref[0].astype(F32)
    buf_ref[halo + t:2 * halo + t, :] = jnp.where(zero_next, 0.0, next_ref[0].astype(F32))
    pad = SSM_CONV // 2
    rc = 64
    for r in range(t // rc):
        for c in range(SSM_XBC // LANES):
            cs = slice(c * LANES, (c + 1) * LANES)
            acc = jnp.zeros((rc, LANES), F32)
            for k in range(SSM_CONV):
                off = halo + r * rc + k - pad
                acc = acc + w_ref[k:k + 1, cs] * buf_ref[off:off + rc, cs]
            o_ref[0, r * rc:(r + 1) * rc, cs] = _silu(acc + b_ref[:, cs])


def _ssmconv(u, w, bias, n_ctx):
    b, nt, _ = u.shape
    t = 256
    n_tiles = nt // t
    hb = t // 16
    last_hb = nt // 16 - 1
    cb = COL_XBC // SSM_XBC
    return pl.pallas_call(
        functools.partial(_ssmconv_kernel, t=t, n_ctx=n_ctx, n_tiles=n_tiles),
        out_shape=jax.ShapeDtypeStruct((b, nt, SSM_XBC), F32),
        grid=(b, n_tiles),
        in_specs=[pl.BlockSpec((1, t, SSM_XBC), lambda bi, i: (bi, i, cb)),
                  pl.BlockSpec((1, 16, SSM_XBC), lambda bi, i: (bi, jnp.maximum(i * hb - 1, 0), cb)),
                  pl.BlockSpec((1, 16, SSM_XBC), lambda bi, i: (bi, jnp.minimum((i + 1) * hb, last_hb), cb)),
                  pl.BlockSpec((SSM_CONV, SSM_XBC), lambda bi, i: (0, 0)),
                  pl.BlockSpec((1, SSM_XBC), lambda bi, i: (0, 0))],
        out_specs=pl.BlockSpec((1, t, SSM_XBC), lambda bi, i: (bi, i, 0)),
        scratch_shapes=[pltpu.VMEM((t + 32, SSM_XBC), F32)],
        compiler_params=_cparams(("parallel", "parallel")),
    )(u, u, u, w, bias.reshape(1, -1))


SCAN_CHUNKS = 2


def _chunk_index(s, reverse, n_chunks, n_ctx_chunks):
    if not reverse:
        return s
    return jnp.where(s < n_ctx_chunks, n_ctx_chunks - 1 - s, n_chunks + n_ctx_chunks - 1 - s)


def _softplus(v):
    return jnp.maximum(v, 0.0) + jnp.log(1.0 + jnp.exp(-jnp.abs(v)))


def _ssd_chunk(refs, rows, reverse):
    if reverse:
        xa_ref, dt_ref, dtb_ref, alog_ref, yf_ref, z_ref, dskip_ref, ng_ref, o_ref, h_ref = refs
    else:
        xa_ref, dt_ref, dtb_ref, alog_ref, o_ref, h_ref = refs
    L = CHUNK
    xa = xa_ref[0, rows, :]
    dt = _softplus(dt_ref[0, rows, :] + dtb_ref[...])
    la = dt * (-jnp.exp(alog_ref[...]))
    r = lax.broadcasted_iota(jnp.int32, (L, L), 0)
    c = lax.broadcasted_iota(jnp.int32, (L, L), 1)
    keep = (r <= c) if reverse else (r >= c)
    cum = jnp.dot(keep.astype(F32), la, preferred_element_type=F32, precision=HIGHEST)
    cum_t = cum.T
    off = SSM_HEADS if reverse else 0
    tot_row = 0 if reverse else L - 1
    hpg = SSM_HEADS // SSM_GROUPS
    gw = hpg * SSM_HEAD_DIM
    lane_head = lax.broadcasted_iota(jnp.int32, (L, gw), 1) // SSM_HEAD_DIM
    ys = []
    for g in range(SSM_GROUPS):
        b_f = xa[:, SSM_INNER + g * SSM_STATE:SSM_INNER + (g + 1) * SSM_STATE]
        c_b = xa[:, SSM_INNER + (SSM_GROUPS + g) * SSM_STATE:SSM_INNER + (SSM_GROUPS + g + 1) * SSM_STATE].astype(BF16)
        gram = lax.dot_general(c_b, b_f.astype(BF16), (((1,), (1,)), ((), ())), preferred_element_type=F32)
        b_t = b_f.T.astype(BF16)

        def expand(mat):
            out = jnp.broadcast_to(mat[:, off + g * hpg + hpg - 1:off + g * hpg + hpg], (L, gw))
            for j in range(hpg - 2, -1, -1):
                out = jnp.where(lane_head == j, mat[:, off + g * hpg + j:off + g * hpg + j + 1], out)
            return out

        dt_l = expand(dt)
        cum_l = expand(cum)
        tot_l = cum_l[tot_row:tot_row + 1, :]
        xdt = xa[:, g * gw:(g + 1) * gw] * dt_l
        xdt_b = xdt.astype(BF16)
        ms = []
        for j in range(hpg):
            hh = off + g * hpg + j
            seg = cum[:, hh:hh + 1] - cum_t[hh:hh + 1, :]
            dec = jnp.exp(jnp.where(keep, seg, -jnp.inf))
            ms.append((gram * dec).astype(BF16))
        rr = jnp.dot(jnp.concatenate(ms, axis=0), xdt_b, preferred_element_type=F32)
        y = jnp.where(lane_head == 0, rr[0:L], 0.0)
        for j in range(1, hpg):
            y = y + jnp.where(lane_head == j, rr[j * L:(j + 1) * L], 0.0)
        hg = h_ref[g]
        y = y + jnp.dot(c_b, hg.astype(BF16), preferred_element_type=F32) * jnp.exp(cum_l)
        wm = jnp.exp(tot_l - cum_l)
        h_ref[g] = jnp.exp(tot_l) * hg + jnp.dot(b_t, (xdt * wm).astype(BF16), preferred_element_type=F32)
        ys.append(y)
    y = jnp.concatenate(ys, axis=1)
    if not reverse:
        o_ref[0, rows, :] = y
        return
    y = yf_ref[0, rows, :] + y + dskip_ref[...] * xa[:, :SSM_INNER]
    y = y * _silu(z_ref[0, rows, :].astype(F32))
    outs = []
    for g in range(SSM_GROUPS):
        yy = y[:, g * gw:(g + 1) * gw]
        outs.append(yy * lax.rsqrt(jnp.mean(yy * yy, axis=-1, keepdims=True) + EPS) * ng_ref[:, g * gw:(g + 1) * gw])
    o_ref[0, rows, :] = jnp.concatenate(outs, axis=1).astype(o_ref.dtype)


def _ssd_specs(xa, u, dt_raw, dt_bias, a_log, yf, d_skip, norm_g, cidx, blk, reverse):
    pad16 = lambda p: jnp.pad(p.reshape(1, -1), ((0, 0), (0, LANES - 2 * SSM_HEADS)))
    in_specs = [pl.BlockSpec((1, blk, SSM_XBC), lambda bi, s: (bi, cidx(s), 0)),
                pl.BlockSpec((1, blk, LANES), lambda bi, s: (bi, cidx(s), 0)),
                pl.BlockSpec((1, LANES), lambda bi, s: (0, 0)),
                pl.BlockSpec((1, LANES), lambda bi, s: (0, 0))]
    args = [xa, dt_raw, pad16(dt_bias), pad16(a_log)]
    if reverse:
        in_specs += [pl.BlockSpec((1, blk, SSM_INNER), lambda bi, s: (bi, cidx(s), 0)),
                     pl.BlockSpec((1, blk, SSM_INNER), lambda bi, s: (bi, cidx(s), COL_Z // SSM_INNER)),
                     pl.BlockSpec((1, SSM_INNER), lambda bi, s: (0, 0)),
                     pl.BlockSpec((1, SSM_INNER), lambda bi, s: (0, 0))]
        args += [yf, u, jnp.repeat(d_skip, SSM_HEAD_DIM).reshape(1, -1), norm_g.reshape(1, -1)]
    return in_specs, args


def _ret_rope(v, cos, sin):
    lo, hi = v[:, :LANES], v[:, LANES:]
    return jnp.concatenate([lo * cos - hi * sin, lo * sin + hi * cos], axis=1)


def _ret_tables(refs, tab_ref, dc_ref, reverse):
    L = CHUNK
    lg = -jnp.exp(refs[5][...])
    t = lax.broadcasted_iota(jnp.int32, (L, 1), 0).astype(F32)
    if reverse:
        tab_ref[:, :RET_INNER] = jnp.exp(lg * (L - t))
        tab_ref[:, RET_INNER:] = jnp.exp(lg * t)
        return
    tab_ref[:, :RET_INNER] = jnp.exp(lg * (t + 1.0))
    tab_ref[:, RET_INNER:] = jnp.exp(lg * (L - 1.0 - t))
    lgb = -jnp.exp(refs[6][...])
    dl = (lax.broadcasted_iota(jnp.int32, (L, L), 0) - lax.broadcasted_iota(jnp.int32, (L, L), 1)).astype(F32)
    for h in range(RET_HEADS):
        vs = slice(h * RET_V_DIM, (h + 1) * RET_V_DIM)
        dc_ref[h] = jnp.where(dl > 0, jnp.exp(lg[:, vs] * jnp.maximum(dl, 0.0)),
                              jnp.where(dl < 0, jnp.exp(lgb[:, vs] * jnp.maximum(-dl, 0.0)), 2.0))


def _ret_chunk(refs, tab_ref, dc_ref, rows, reverse):
    if reverse:
        (q_ref, k_ref, v_ref, cos_ref, sin_ref, dec_ref, yf_ref, g_ref, gng_ref, gnb_ref, o_ref, h_ref) = refs
    else:
        (q_ref, k_ref, v_ref, cos_ref, sin_ref, dec_ref, decb_ref, o_ref, h_ref) = refs
    L = CHUNK
    qk = RET_HEADS * RET_QK_DIM
    cos, sin = cos_ref[rows, :], sin_ref[rows, :]
    q = _ret_rope(q_ref[0, rows, :].astype(F32), cos, sin)
    k = _ret_rope(k_ref[0, rows, :].astype(F32), cos, sin) * (RET_QK_DIM ** -0.5)
    v = v_ref[0, rows, :].astype(F32)
    q_b = q.astype(BF16)
    k_t = k.T.astype(BF16)
    lg = -jnp.exp(dec_ref[...])
    e_in = tab_ref[:, :RET_INNER]
    w_st = tab_ref[:, RET_INNER:]
    hs = h_ref[...]
    y = jnp.dot(q_b, hs.astype(BF16), preferred_element_type=F32) * e_in
    upd = jnp.dot(k_t, (v * w_st).astype(BF16), preferred_element_type=F32)
    row_head = (lax.broadcasted_iota(jnp.int32, (qk, RET_INNER), 0) % LANES) // (RET_QK_DIM // 2)
    lane_head = lax.broadcasted_iota(jnp.int32, (qk, RET_INNER), 1) // RET_V_DIM
    h_ref[...] = jnp.exp(lg * float(L)) * hs + jnp.where(row_head == lane_head, upd, 0.0)

    if not reverse:
        q_head = (lax.broadcasted_iota(jnp.int32, (L, qk), 1) % LANES) // (RET_QK_DIM // 2)
        qs = jnp.concatenate([jnp.where(q_head == h, q, 0.0) for h in range(RET_HEADS)], axis=0).astype(BF16)
        sc = jnp.dot(qs, k_t, preferred_element_type=F32)
        parts = []
        for h in range(RET_HEADS):
            vs = slice(h * RET_V_DIM, (h + 1) * RET_V_DIM)
            p = (sc[h * L:(h + 1) * L] * dc_ref[h]).astype(BF16)
            parts.append(jnp.dot(p, v[:, vs].astype(BF16), preferred_element_type=F32))
        o_ref[0, rows, :] = y + jnp.concatenate(parts, axis=1)
        return
    y = yf_ref[0, rows, :] + y
    outs = []
    for h in range(RET_HEADS):
        vs = slice(h * RET_V_DIM, (h + 1) * RET_V_DIM)
        yy = y[:, vs]
        mu = jnp.mean(yy, axis=-1, keepdims=True)
        var = jnp.mean(jnp.square(yy - mu), axis=-1, keepdims=True)
        outs.append((yy - mu) * lax.rsqrt(var + EPS) * gng_ref[:, vs] + gnb_ref[:, vs])
    o_ref[0, rows, :] = (_silu(g_ref[0, rows, :].astype(F32)) * jnp.concatenate(outs, axis=1)).astype(o_ref.dtype)


def _ret_specs(u, cos4, sin4, ret_decay, yf, gn_g, gn_b, cidx, blk, reverse):
    qk = RET_HEADS * RET_QK_DIM
    lane_dec = lambda d: jnp.repeat(d, RET_V_DIM).reshape(1, -1)
    in_specs = [pl.BlockSpec((1, blk, qk), lambda bi, s: (bi, cidx(s), COL_RQ // qk)),
                pl.BlockSpec((1, blk, qk), lambda bi, s: (bi, cidx(s), COL_RK // qk)),
                pl.BlockSpec((1, blk, RET_INNER), lambda bi, s: (bi, cidx(s), COL_RV // RET_INNER)),
                pl.BlockSpec((blk, LANES), lambda bi, s: (cidx(s), 0)),
                pl.BlockSpec((blk, LANES), lambda bi, s: (cidx(s), 0)),
                pl.BlockSpec((1, RET_INNER), lambda bi, s: (0, 0))]
    args = [u, u, u, cos4, sin4, lane_dec(ret_decay[1] if reverse else ret_decay[0])]
    if reverse:
        in_specs += [pl.BlockSpec((1, blk, RET_INNER), lambda bi, s: (bi, cidx(s), 0)),
                     pl.BlockSpec((1, blk, RET_INNER), lambda bi, s: (bi, cidx(s), COL_RG // RET_INNER)),
                     pl.BlockSpec((1, RET_INNER), lambda bi, s: (0, 0)),
                     pl.BlockSpec((1, RET_INNER), lambda bi, s: (0, 0))]
        args += [yf, u, gn_g.reshape(1, -1), gn_b.reshape(1, -1)]
    else:
        in_specs += [pl.BlockSpec((1, RET_INNER), lambda bi, s: (0, 0))]
        args += [lane_dec(ret_decay[1])]
    return in_specs, args


def _scan_kernel(*refs, n_ssd_in, reverse):
    n_in = len(refs) - 6
    ssd_o, ret_o, ssd_h, ret_h, ret_tab, ret_dc = refs[n_in:]
    ssd_refs = refs[:n_ssd_in] + (ssd_o, ssd_h)
    ret_refs = refs[n_ssd_in:n_in] + (ret_o, ret_h)

    @pl.when(pl.program_id(1) == 0)
    def _():
        ssd_h[...] = jnp.zeros_like(ssd_h)
        ret_h[...] = jnp.zeros_like(ret_h)
        _ret_tables(ret_refs, ret_tab, ret_dc, reverse)

    order = range(SCAN_CHUNKS - 1, -1, -1) if reverse else range(SCAN_CHUNKS)
    for ci in order:
        rows = slice(ci * CHUNK, (ci + 1) * CHUNK)
        _ssd_chunk(ssd_refs, rows, reverse)
        _ret_chunk(ret_refs, ret_tab, ret_dc, rows, reverse)


def _scan(ssd_in, ret_in, n_ctx, reverse):
    b, nt, _ = ssd_in[0].shape
    blk = SCAN_CHUNKS * CHUNK
    cidx = functools.partial(_chunk_index, reverse=reverse, n_chunks=nt // blk, n_ctx_chunks=n_ctx // blk)
    ssd_specs, ssd_args = _ssd_specs(*ssd_in, cidx, blk, reverse)
    ret_specs, ret_args = _ret_specs(*ret_in, cidx, blk, reverse)
    out_spec = lambda w: pl.BlockSpec((1, blk, w), lambda bi, s: (bi, cidx(s), 0))
    out_dtype = BF16 if reverse else F32
    return pl.pallas_call(
        functools.partial(_scan_kernel, n_ssd_in=len(ssd_args), reverse=reverse),
        out_shape=(jax.ShapeDtypeStruct((b, nt, SSM_INNER), out_dtype), jax.ShapeDtypeStruct((b, nt, RET_INNER), out_dtype)),
        grid=(b, nt // blk),
        in_specs=ssd_specs + ret_specs,
        out_specs=(out_spec(SSM_INNER), out_spec(RET_INNER)),
        scratch_shapes=[pltpu.VMEM((SSM_GROUPS, SSM_STATE, SSM_INNER // SSM_GROUPS), F32),
                        pltpu.VMEM((RET_HEADS * RET_QK_DIM, RET_INNER), F32),
                        pltpu.VMEM((CHUNK, 2 * RET_INNER), F32),
                        pltpu.VMEM((RET_HEADS, CHUNK, CHUNK), F32)],
        compiler_params=_cparams(("parallel", "arbitrary")),
    )(*ssd_args, *ret_args)


def _mla_rope(v, c, s1, s2):
    return v * c + pltpu.roll(v, 16, 1) * s1 + pltpu.roll(v, LANES - 16, 1) * s2


def _rms(v, g):
    return v * lax.rsqrt(jnp.mean(v * v, axis=-1, keepdims=True) + EPS) * g


def _mlaprep_kernel(cq_ref, ckv_ref, kr_ref, qg_ref, kvg_ref, wq_ref, wk_ref, wv_ref, c_ref, s1_ref, s2_ref,
                    qt_out, k_out, vt_out):
    c, s1, s2 = c_ref[...], s1_ref[...], s2_ref[...]
    scale = (MLA_NOPE + MLA_ROPE) ** -0.5 * math.log2(math.e)
    q = jnp.dot(_rms(cq_ref[0].astype(F32), qg_ref[...]).astype(BF16), wq_ref[...], preferred_element_type=F32)
    ckv = _rms(ckv_ref[0].astype(F32), kvg_ref[...]).astype(BF16)
    kn = jnp.dot(ckv, wk_ref[...], preferred_element_type=F32)
    vt_out[0] = jnp.dot(ckv, wv_ref[...], preferred_element_type=F32).T.astype(BF16)
    krr = _mla_rope(kr_ref[0].astype(F32), c, s1, s2)
    for h in range(MLA_HEADS):
        hs = slice(h * LANES, (h + 1) * LANES)
        qt_out[0, hs, :] = (_mla_rope(q[:, hs], c, s1, s2) * scale).T.astype(BF16)
        k_out[0, :, hs] = (kn[:, hs] + krr).astype(BF16)


def _mlaprep(u, q_g, kv_g, wq, wk, wv, ctab, s1tab, s2tab):
    b, nt, _ = u.shape
    tm = _tile(nt, 1280, 256)
    hw = MLA_HEADS * LANES
    const = lambda shape: pl.BlockSpec(shape, lambda bi, i: (0, 0))
    return pl.pallas_call(
        _mlaprep_kernel,
        out_shape=(jax.ShapeDtypeStruct((b, hw, nt), BF16), jax.ShapeDtypeStruct((b, nt, hw), BF16),
                   jax.ShapeDtypeStruct((b, MLA_HEADS * MLA_V, nt), BF16)),
        grid=(b, nt // tm),
        in_specs=[pl.BlockSpec((1, tm, MLA_Q_RANK), lambda bi, i: (bi, i, COL_CQ // MLA_Q_RANK)),
                  pl.BlockSpec((1, tm, MLA_KV_RANK), lambda bi, i: (bi, i, COL_CKV // MLA_KV_RANK)),
                  pl.BlockSpec((1, tm, LANES), lambda bi, i: (bi, i, COL_KR // LANES)),
                  const((1, MLA_Q_RANK)), const((1, MLA_KV_RANK)),
                  const((MLA_Q_RANK, hw)), const((MLA_KV_RANK, hw)), const((MLA_KV_RANK, MLA_HEADS * MLA_V)),
                  pl.BlockSpec((tm, LANES), lambda bi, i: (i, 0)),
                  pl.BlockSpec((tm, LANES), lambda bi, i: (i, 0)),
                  pl.BlockSpec((tm, LANES), lambda bi, i: (i, 0))],
        out_specs=(pl.BlockSpec((1, hw, tm), lambda bi, i: (bi, 0, i)),
                   pl.BlockSpec((1, tm, hw), lambda bi, i: (bi, i, 0)),
                   pl.BlockSpec((1, MLA_HEADS * MLA_V, tm), lambda bi, i: (bi, 0, i))),
        compiler_params=_cparams(("parallel", "parallel")),
    )(u, u, u, q_g.reshape(1, -1), kv_g.reshape(1, -1), wq, wk, wv, ctab, s1tab, s2tab)


ACC_ROWS = MLA_V + 16


def _flash_kernel(k_ref, qt_ref, vt_ref, o_ref, m_ref, acc_ref, *, tq, tk, n_ctx, nk):
    i = pl.program_id(2)
    j = pl.program_id(3)

    @pl.when(j == 0)
    def _():
        m_ref[...] = jnp.full_like(m_ref, -jnp.inf)
        acc_ref[...] = jnp.zeros_like(acc_ref)

    def step(masked):
        ones = jnp.ones((ACC_ROWS - MLA_V, tk), BF16)
        for hh in range(2):
            hs = slice(hh * LANES, (hh + 1) * LANES)
            s = jnp.dot(k_ref[0, :, hs], qt_ref[0, hs, :], preferred_element_type=F32)
            if masked:
                keys = j * tk + lax.broadcasted_iota(jnp.int32, (tk, n_ctx), 0)
                s_ctx = jnp.where(keys >= n_ctx, -1e30, s[:, :n_ctx])
                s = s_ctx if tq == n_ctx else jnp.concatenate([s_ctx, s[:, n_ctx:]], axis=1)
            m_prev = m_ref[hh]
            m_new = jnp.maximum(m_prev, jnp.max(s, axis=0, keepdims=True))
            alpha = jnp.exp2(m_prev - m_new)
            p = jnp.exp2(s - m_new).astype(BF16)
            vs = slice(hh * MLA_V, (hh + 1) * MLA_V)
            lhs = jnp.concatenate([vt_ref[0, vs, :], ones], axis=0)
            acc_ref[hh] = alpha * acc_ref[hh] + jnp.dot(lhs, p, preferred_element_type=F32)
            m_ref[hh] = m_new

    @pl.when(i * tq < n_ctx)
    def _():
        step(True)

    @pl.when(i * tq >= n_ctx)
    def _():
        step(False)

    @pl.when(j == nk - 1)
    def _():
        outs = []
        for hh in range(2):
            a = acc_ref[hh]
            outs.append(a[:MLA_V] * (1.0 / a[MLA_V:MLA_V + 1]))
        o_ref[0] = jnp.concatenate(outs, axis=0).T.astype(o_ref.dtype)


def _flash(k, qt, vt, n_ctx, tq_target=3328, tk_target=1664):
    b, nt, _ = k.shape
    tq = _tile(nt, tq_target, LANES)
    tk = _tile(nt, tk_target, LANES)
    assert tq >= n_ctx and n_ctx % LANES == 0
    nk = nt // tk
    pairs = MLA_HEADS // 2
    return pl.pallas_call(
        functools.partial(_flash_kernel, tq=tq, tk=tk, n_ctx=n_ctx, nk=nk),
        out_shape=jax.ShapeDtypeStruct((b, nt, MLA_HEADS * MLA_V), BF16),
        grid=(b, pairs, nt // tq, nk),
        in_specs=[pl.BlockSpec((1, tk, 2 * LANES), lambda bi, p, i, j: (bi, j, p)),
                  pl.BlockSpec((1, 2 * LANES, tq), lambda bi, p, i, j: (bi, p, i)),
                  pl.BlockSpec((1, 2 * MLA_V, tk), lambda bi, p, i, j: (bi, p, j))],
        out_specs=pl.BlockSpec((1, tq, 2 * MLA_V), lambda bi, p, i, j: (bi, i, p)),
        scratch_shapes=[pltpu.VMEM((2, 1, tq), F32), pltpu.VMEM((2, ACC_ROWS, tq), F32)],
        compiler_params=_cparams(("parallel", "parallel", "parallel", "arbitrary")),
    )(k, qt, vt)


def _merge_kernel(xs_ref, gl_ref, b0_ref, b1_ref, b2_ref, b3_ref, wb_ref, wo_ref, mod_ref, o_ref, *, tm, n_ctx):
    i = pl.program_id(1)
    merged = None
    for n, br in enumerate((b0_ref, b1_ref, b2_ref, b3_ref)):
        proj = jnp.dot(br[0], wb_ref[n], preferred_element_type=F32)
        term = jax.nn.sigmoid(gl_ref[0, :, n * D_MODEL:(n + 1) * D_MODEL].astype(F32)) * proj
        merged = term if merged is None else merged + term
    out = jnp.dot(merged.astype(BF16), wo_ref[...], preferred_element_type=F32)
    o_ref[0] = xs_ref[0] + _res_gate(mod_ref[0], i * tm, tm, n_ctx, 0) * out


def _merge(xs, u, branches, wb, wo, modtab, n_ctx):
    b, nt, d = xs.shape
    tm = 256
    row = lambda w: pl.BlockSpec((1, tm, w), lambda bi, i: (bi, i, 0))
    return pl.pallas_call(
        functools.partial(_merge_kernel, tm=tm, n_ctx=n_ctx),
        out_shape=jax.ShapeDtypeStruct((b, nt, d), F32),
        grid=(b, nt // tm),
        in_specs=[row(d), row(N_BRANCH * d), row(BRANCH_DIM), row(BRANCH_DIM), row(BRANCH_DIM), row(BRANCH_DIM),
                  pl.BlockSpec((N_BRANCH, BRANCH_DIM, d), lambda bi, i: (0, 0, 0)),
                  pl.BlockSpec((d, d), lambda bi, i: (0, 0)),
                  pl.BlockSpec((1, 16, d), lambda bi, i: (bi, 0, 0))],
        out_specs=row(d),
        compiler_params=_cparams(("parallel", "parallel")),
    )(xs, u, *branches, wb, wo, modtab)


def _ffn_kernel(xs_ref, mod_ref, g_ref, wa_ref, wg_ref, wo_ref, o_ref, h_ref, acc_ref, *, tm, n_ctx, nj):
    i = pl.program_id(1)
    j = pl.program_id(2)

    @pl.when(j == 0)
    def _():
        h_ref[...] = _norm_mod(xs_ref[0], mod_ref[0], g_ref[...], i * tm, n_ctx, 1).astype(BF16)

    h = h_ref[...]
    a = jnp.dot(h, wa_ref[...], preferred_element_type=F32)
    gate = jnp.dot(h, wg_ref[...], preferred_element_type=F32)
    part = jnp.dot((_silu(gate) * a).astype(BF16), wo_ref[...], preferred_element_type=F32)

    @pl.when(j == 0)
    def _():
        acc_ref[...] = part

    @pl.when(j > 0)
    def _():
        acc_ref[...] += part

    @pl.when(j == nj - 1)
    def _():
        o_ref[0] = xs_ref[0] + _res_gate(mod_ref[0], i * tm, tm, n_ctx, 1) * acc_ref[...]


def _ffn(xs, modtab, g, w_in, w_out, n_ctx):
    b, nt, d = xs.shape
    f = w_out.shape[0]
    tm = _tile(nt, 640, 128)
    tf = _tile(f, 1408, LANES)
    nj = f // tf
    return pl.pallas_call(
        functools.partial(_ffn_kernel, tm=tm, n_ctx=n_ctx, nj=nj),
        out_shape=jax.ShapeDtypeStruct((b, nt, d), F32),
        grid=(b, nt // tm, nj),
        in_specs=[pl.BlockSpec((1, tm, d), lambda bi, i, j: (bi, i, 0)),
                  pl.BlockSpec((1, 16, d), lambda bi, i, j: (bi, 0, 0)),
                  pl.BlockSpec((1, d), lambda bi, i, j: (0, 0)),
                  pl.BlockSpec((d, tf), lambda bi, i, j: (0, j)),
                  pl.BlockSpec((d, tf), lambda bi, i, j: (0, nj + j)),
                  pl.BlockSpec((tf, d), lambda bi, i, j: (j, 0))],
        out_specs=pl.BlockSpec((1, tm, d), lambda bi, i, j: (bi, i, 0)),
        scratch_shapes=[pltpu.VMEM((tm, d), BF16), pltpu.VMEM((tm, d), F32)],
        compiler_params=_cparams(("parallel", "parallel", "arbitrary")),
    )(xs, modtab, g.reshape(1, d), w_in, w_in, w_out)


def _final_kernel(x_ref, g_ref, o_ref):
    o_ref[0] = _rms(x_ref[0], g_ref[...])


def _final_norm(xs, g, n_ctx):
    b, nt, d = xs.shape
    t = 256
    skip = n_ctx // t
    return pl.pallas_call(
        _final_kernel,
        out_shape=jax.ShapeDtypeStruct((b, nt - n_ctx, d), F32),
        grid=(b, (nt - n_ctx) // t),
        in_specs=[pl.BlockSpec((1, t, d), lambda bi, i: (bi, i + skip, 0)),
                  pl.BlockSpec((1, d), lambda bi, i: (0, 0))],
        out_specs=pl.BlockSpec((1, t, d), lambda bi, i: (bi, i, 0)),
        compiler_params=_cparams(("parallel", "parallel")),
    )(xs, g.reshape(1, d))


def _prep_w_in(w):
    d = w.shape[0]
    o_conv, o_z, o_xbc, o_dt = 0, 1024, 1536, 2560
    o_rq, o_rk, o_rv, o_rg = 2576, 2832, 3088, 3600
    o_cq, o_ckv, o_kr, o_gl = 4112, 4496, 4752, 4784
    seg = lambda o, n: w[:, o:o + n]

    def halves(o):
        s = seg(o, RET_HEADS * RET_QK_DIM).reshape(d, RET_HEADS, 2, RET_QK_DIM // 2)
        return jnp.transpose(s, (0, 2, 1, 3)).reshape(d, RET_HEADS * RET_QK_DIM)

    zeros = lambda n: jnp.zeros((d, n), w.dtype)
    parts = [seg(o_gl, N_BRANCH * D_MODEL), seg(o_conv, 2 * CONV_DIM), seg(o_xbc, SSM_XBC), seg(o_z, SSM_INNER),
             seg(o_rv, RET_INNER), seg(o_rg, RET_INNER), halves(o_rq), halves(o_rk),
             seg(o_ckv, MLA_KV_RANK), seg(o_cq, MLA_Q_RANK),
             seg(o_dt, 2 * SSM_HEADS), zeros(LANES - 2 * SSM_HEADS),
             zeros(MLA_NOPE), seg(o_kr, MLA_ROPE), zeros(LANES - MLA_NOPE - MLA_ROPE),
             zeros(LANES)]
    out = jnp.concatenate(parts, axis=1).astype(BF16)
    assert out.shape[1] == N_IN_PAD
    return out


def _prep_mla_w(w_uq, w_ukv):
    rq, rkv = w_uq.shape[0], w_ukv.shape[0]
    hd = MLA_NOPE + MLA_ROPE
    wq = jnp.pad(w_uq.reshape(rq, MLA_HEADS, hd), ((0, 0), (0, 0), (0, LANES - hd))).reshape(rq, MLA_HEADS * LANES)
    kv = w_ukv.reshape(rkv, MLA_HEADS, MLA_NOPE + MLA_V)
    wk = jnp.pad(kv[:, :, :MLA_NOPE], ((0, 0), (0, 0), (0, LANES - MLA_NOPE))).reshape(rkv, MLA_HEADS * LANES)
    wv = kv[:, :, MLA_NOPE:].reshape(rkv, MLA_HEADS * MLA_V)
    return wq.astype(BF16), wk.astype(BF16), wv.astype(BF16)


def _axial_angles(n, rot_dim):
    rows = n // GRID_W
    row = jnp.repeat(jnp.arange(rows, dtype=F32), GRID_W)
    col = jnp.tile(jnp.arange(GRID_W, dtype=F32), rows)
    nf = rot_dim // 4
    inv = ROPE_BASE ** (-jnp.arange(nf, dtype=F32) / nf)
    return jnp.concatenate([row[:, None] * inv, col[:, None] * inv], axis=-1)


def _rope_tables(n_lat, n_ctx):
    ang = _axial_angles(n_lat, RET_QK_DIM)
    cos4 = jnp.tile(jnp.cos(ang), (1, RET_HEADS))
    sin4 = jnp.tile(jnp.sin(ang), (1, RET_HEADS))
    cos4 = jnp.concatenate([jnp.ones((n_ctx, LANES), F32), cos4], axis=0)
    sin4 = jnp.concatenate([jnp.zeros((n_ctx, LANES), F32), sin4], axis=0)
    ang = _axial_angles(n_lat, MLA_ROPE)
    c, s = jnp.cos(ang), jnp.sin(ang)
    half = MLA_ROPE // 2
    one = jnp.ones((n_lat, MLA_NOPE), F32)
    zero = lambda w: jnp.zeros((n_lat, w), F32)
    tail = LANES - MLA_NOPE - MLA_ROPE
    ctab = jnp.concatenate([one, c, c, jnp.ones((n_lat, tail), F32)], axis=1)
    s1 = jnp.concatenate([zero(MLA_NOPE + half), s, zero(tail)], axis=1)
    s2 = jnp.concatenate([zero(MLA_NOPE), -s, zero(half + tail)], axis=1)
    ctab = jnp.concatenate([jnp.ones((n_ctx, LANES), F32), ctab], axis=0)
    s1 = jnp.concatenate([jnp.zeros((n_ctx, LANES), F32), s1], axis=0)
    s2 = jnp.concatenate([jnp.zeros((n_ctx, LANES), F32), s2], axis=0)
    return cos4, sin4, ctab, s1, s2


def kernel(x, c, ctx, c_ctx, w_ada, b_ada, norm1_g, norm2_g, w_in, conv_w, conv_b, conv_ln_g, conv_ln_b,
           ssm_conv_w, ssm_conv_b, ssm_dt_bias, ssm_a_log, ssm_d, ssm_norm_g, ret_decay, ret_gn_g, ret_gn_b,
           mla_q_norm_g, mla_kv_norm_g, mla_w_uq, mla_w_ukv, w_branch, w_out, w_ffn_in, w_ffn_out, final_norm_g):
    batch, n_lat, d = x.shape
    n_ctx = ctx.shape[1]
    depth = w_in.shape[0]
    assert d == D_MODEL and n_ctx % 256 == 0 and n_lat % 256 == 0 and batch + 1 <= 8
    assert n_ctx % (SCAN_CHUNKS * CHUNK) == 0
    cos4, sin4, ctab, s1tab, s2tab = _rope_tables(n_lat, n_ctx)
    xs = jnp.concatenate([ctx, x], axis=1)
    cvec = jnp.zeros((8, d), F32).at[:batch].set(c).at[batch].set(c_ctx)
    for i in range(depth):
        mod = _ada(cvec, w_ada[i], b_ada[i]).reshape(8, 6, d)
        modtab = jnp.zeros((batch, 16, d), F32)
        modtab = modtab.at[:, 0:6].set(jnp.broadcast_to(mod[batch], (batch, 6, d))).at[:, 8:14].set(mod[:batch])
        u, dt_raw = _inproj(xs, modtab, norm1_g[i], _prep_w_in(w_in[i]), n_ctx)
        conv_y = _convmod(u, conv_w[i], conv_b[i], conv_ln_g[i], conv_ln_b[i], n_ctx)
        xa = _ssmconv(u, ssm_conv_w[i], ssm_conv_b[i], n_ctx)
        ssm_f, ret_f = _scan((xa, u, dt_raw, ssm_dt_bias[i], ssm_a_log[i], None, None, None),
                             (u, cos4, sin4, ret_decay[i], None, None, None), n_ctx, False)
        ssm_y, ret_y = _scan((xa, u, dt_raw, ssm_dt_bias[i], ssm_a_log[i], ssm_f, ssm_d[i], ssm_norm_g[i]),
                             (u, cos4, sin4, ret_decay[i], ret_f, ret_gn_g[i], ret_gn_b[i]), n_ctx, True)
        wq, wk, wv = _prep_mla_w(mla_w_uq[i], mla_w_ukv[i])
        qt, k, vt = _mlaprep(u, mla_q_norm_g[i], mla_kv_norm_g[i], wq, wk, wv, ctab, s1tab, s2tab)
        att = _flash(k, qt, vt, n_ctx)
        xs = _merge(xs, u, (conv_y, ssm_y, ret_y, att), w_branch[i].astype(BF16), w_out[i].astype(BF16),
                    modtab, n_ctx)
        xs = _ffn(xs, modtab, norm2_g[i], w_ffn_in[i].astype(BF16), w_ffn_out[i].astype(BF16), n_ctx)
    return _final_norm(xs, final_norm_g, n_ctx)
```

```python
import functools
import math

import jax
import jax.numpy as jnp
from jax import lax
from jax.experimental import pallas as pl
from jax.experimental.pallas import tpu as pltpu

F32 = jnp.float32
BF16 = jnp.bfloat16
HIGHEST = lax.Precision.HIGHEST

D_MODEL = 1024
GRID_W = 64
CHUNK = 128
ROPE_BASE = 10000.0
EPS = 1e-6
BRANCH_DIM = D_MODEL // 2
N_BRANCH = 4
CONV_DIM = BRANCH_DIM
CONV_WIDTH = 31
SSM_INNER = BRANCH_DIM
SSM_HEAD_DIM = 64
SSM_HEADS = SSM_INNER // SSM_HEAD_DIM
SSM_GROUPS = 2
SSM_STATE = 128
SSM_CONV = 5
SSM_XBC = SSM_INNER + 2 * SSM_GROUPS * SSM_STATE
RET_HEADS = 4
RET_QK_DIM = 64
RET_INNER = BRANCH_DIM
RET_V_DIM = RET_INNER // RET_HEADS
MLA_HEADS = 8
MLA_NOPE = 64
MLA_ROPE = 32
MLA_V = BRANCH_DIM // MLA_HEADS
MLA_Q_RANK = 384
MLA_KV_RANK = 256
FFN_DIM = ((8 * D_MODEL // 3 + 255) // 256) * 256

LANES = 128
SUBLANES = 8
VMEM_LIMIT_BYTES = 56 * 1024 * 1024

COL_GL = 0
COL_CONV = 4096
COL_XBC = 5120
COL_Z = 6144
COL_RV = 6656
COL_RG = 7168
COL_RQ = 7680
COL_RK = 7936
COL_CKV = 8192
COL_CQ = 8448
COL_DT = 8832
COL_KR = 8960
N_IN_PAD = 9216


def _tile(n, target, mult):
    best = None
    for t in range(mult, min(n, target) + 1, mult):
        if n % t == 0:
            best = t
    assert best is not None, (n, target, mult)
    return best


def _cparams(sem):
    return pltpu.CompilerParams(dimension_semantics=sem, vmem_limit_bytes=VMEM_LIMIT_BYTES)


def _silu(v):
    return v * jax.nn.sigmoid(v)


def _norm_mod(x, mod, g, row0, n_ctx, k):
    y = x * lax.rsqrt(jnp.mean(x * x, axis=-1, keepdims=True) + EPS) * g
    rows = row0 + lax.broadcasted_iota(jnp.int32, (x.shape[0], 1), 0)
    is_ctx = rows < n_ctx
    shift = jnp.where(is_ctx, mod[3 * k:3 * k + 1], mod[8 + 3 * k:9 + 3 * k])
    scale = jnp.where(is_ctx, mod[3 * k + 1:3 * k + 2], mod[9 + 3 * k:10 + 3 * k])
    return y * (1.0 + scale) + shift


def _res_gate(mod, row0, n_rows, n_ctx, k):
    rows = row0 + lax.broadcasted_iota(jnp.int32, (n_rows, 1), 0)
    return jnp.where(rows < n_ctx, mod[3 * k + 2:3 * k + 3], mod[10 + 3 * k:11 + 3 * k])


def _ada_kernel(c_ref, w_ref, b_ref, o_ref):
    o_ref[...] = jnp.dot(_silu(c_ref[...]), w_ref[...], preferred_element_type=F32, precision=HIGHEST) + b_ref[...]


def _ada(cvec, w, b):
    n = w.shape[1]
    tn = _tile(n, 1536, LANES)
    return pl.pallas_call(
        _ada_kernel,
        out_shape=jax.ShapeDtypeStruct((cvec.shape[0], n), F32),
        grid=(n // tn,),
        in_specs=[pl.BlockSpec(cvec.shape, lambda j: (0, 0)),
                  pl.BlockSpec((w.shape[0], tn), lambda j: (0, j)),
                  pl.BlockSpec((1, tn), lambda j: (0, j))],
        out_specs=pl.BlockSpec((cvec.shape[0], tn), lambda j: (0, j)),
        compiler_params=_cparams(("arbitrary",)),
    )(cvec, w, b.reshape(1, n))


def _inproj_kernel(x_ref, mod_ref, g_ref, w_ref, o_ref, dt_ref, h_ref, *, tm, tn, n_ctx):
    i = pl.program_id(1)
    j = pl.program_id(2)

    @pl.when(j == 0)
    def _():
        h_ref[...] = _norm_mod(x_ref[0], mod_ref[0], g_ref[...], i * tm, n_ctx, 0).astype(BF16)

    res = jnp.dot(h_ref[...], w_ref[...], preferred_element_type=F32)
    o_ref[0] = res.astype(BF16)

    @pl.when(j == COL_DT // tn)
    def _():
        dt_ref[0] = res[:, COL_DT % tn:COL_DT % tn + LANES]


def _inproj(xs, modtab, g, w, n_ctx):
    b, nt, d = xs.shape
    n = w.shape[1]
    tm = _tile(nt, 1664, LANES)
    tn = _tile(n, 1024, LANES)
    assert COL_DT % tn + LANES <= tn
    return pl.pallas_call(
        functools.partial(_inproj_kernel, tm=tm, tn=tn, n_ctx=n_ctx),
        out_shape=(jax.ShapeDtypeStruct((b, nt, n), BF16), jax.ShapeDtypeStruct((b, nt, LANES), F32)),
        grid=(b, nt // tm, n // tn),
        in_specs=[pl.BlockSpec((1, tm, d), lambda bi, i, j: (bi, i, 0)),
                  pl.BlockSpec((1, 16, d), lambda bi, i, j: (bi, 0, 0)),
                  pl.BlockSpec((1, d), lambda bi, i, j: (0, 0)),
                  pl.BlockSpec((d, tn), lambda bi, i, j: (0, j))],
        out_specs=(pl.BlockSpec((1, tm, tn), lambda bi, i, j: (bi, i, j)),
                   pl.BlockSpec((1, tm, LANES), lambda bi, i, j: (bi, i, 0))),
        scratch_shapes=[pltpu.VMEM((tm, d), BF16)],
        compiler_params=_cparams(("parallel", "parallel", "arbitrary")),
    )(xs, modtab, g.reshape(1, d), w)


def _halo_flags(i, t, n_ctx, n_tiles):
    zero_prev = jnp.logical_or(i == 0, i * t == n_ctx)
    zero_next = jnp.logical_or((i + 1) * t == n_ctx, i == n_tiles - 1)
    return zero_prev, zero_next


def _convmod_kernel(cur_ref, prev_ref, next_ref, w_ref, b_ref, lg_ref, lb_ref, o_ref, buf_ref,
                    *, t, n_ctx, n_tiles):
    i = pl.program_id(1)
    zero_prev, zero_next = _halo_flags(i, t, n_ctx, n_tiles)

    def glu(u):
        u = u.astype(F32)
        return u[:, :CONV_DIM] * jax.nn.sigmoid(u[:, CONV_DIM:])

    halo = 16
    buf_ref[0, 0:halo, :] = jnp.where(zero_prev, 0.0, glu(prev_ref[0]))
    buf_ref[0, halo:halo + t, :] = glu(cur_ref[0])
    buf_ref[0, halo + t:2 * halo + t, :] = jnp.where(zero_next, 0.0, glu(next_ref[0]))
    n_sh = t + 2 * halo - SUBLANES
    for sh in range(1, SUBLANES):
        buf_ref[sh, 0:n_sh, :] = buf_ref[0, sh:sh + n_sh, :]
    pad = CONV_WIDTH // 2
    h = jnp.zeros((t, CONV_DIM), F32)
    for k in range(CONV_WIDTH):
        off = halo + k - pad
        base = off - off % SUBLANES
        h = h + w_ref[k:k + 1, :] * buf_ref[off % SUBLANES, base:base + t, :]
    h = h + b_ref[...]
    mu = jnp.mean(h, axis=-1, keepdims=True)
    var = jnp.mean(jnp.square(h - mu), axis=-1, keepdims=True)
    y = (h - mu) * lax.rsqrt(var + EPS) * lg_ref[...] + lb_ref[...]
    o_ref[0] = _silu(y).astype(o_ref.dtype)


def _convmod(u, w, bias, ln_g, ln_b, n_ctx):
    b, nt, _ = u.shape
    t = 256
    n_tiles = nt // t
    hb = t // 16
    last_hb = nt // 16 - 1
    width = 2 * CONV_DIM
    cb = COL_CONV // width
    return pl.pallas_call(
        functools.partial(_convmod_kernel, t=t, n_ctx=n_ctx, n_tiles=n_tiles),
        out_shape=jax.ShapeDtypeStruct((b, nt, CONV_DIM), BF16),
        grid=(b, n_tiles),
        in_specs=[pl.BlockSpec((1, t, width), lambda bi, i: (bi, i, cb)),
                  pl.BlockSpec((1, 16, width), lambda bi, i: (bi, jnp.maximum(i * hb - 1, 0), cb)),
                  pl.BlockSpec((1, 16, width), lambda bi, i: (bi, jnp.minimum((i + 1) * hb, last_hb), cb)),
                  pl.BlockSpec((CONV_WIDTH, CONV_DIM), lambda bi, i: (0, 0)),
                  pl.BlockSpec((1, CONV_DIM), lambda bi, i: (0, 0)),
                  pl.BlockSpec((1, CONV_DIM), lambda bi, i: (0, 0)),
                  pl.BlockSpec((1, CONV_DIM), lambda bi, i: (0, 0))],
        out_specs=pl.BlockSpec((1, t, CONV_DIM), lambda bi, i: (bi, i, 0)),
        scratch_shapes=[pltpu.VMEM((SUBLANES, t + 32, CONV_DIM), F32)],
        compiler_params=_cparams(("parallel", "parallel")),
    )(u, u, u, w, bias.reshape(1, -1), ln_g.reshape(1, -1), ln_b.reshape(1, -1))


def _ssmconv_kernel(cur_ref, prev_ref, next_ref, w_ref, b_ref, o_ref, buf_ref, *, t, n_ctx, n_tiles):
    i = pl.program_id(1)
    zero_prev, zero_next = _halo_flags(i, t, n_ctx, n_tiles)
    halo = 16
    buf_ref[0:halo, :] = jnp.where(zero_prev, 0.0, prev_ref[0].astype(F32))
    buf_ref[halo:halo + t, :] = cur_ref[0].astype(F32)
    buf_ref[halo + t:2 * halo + t, :] = jnp.where(zero_next, 0.0, next_ref[0].astype(F32))
    pad = SSM_CONV // 2
    rc = 64
    for r in range(t // rc):
        for c in range(SSM_XBC // LANES):
            cs = slice(c * LANES, (c + 1) * LANES)
            acc = jnp.zeros((rc, LANES), F32)
            for k in range(SSM_CONV):
                off = halo + r * rc + k - pad
                acc = acc + w_ref[k:k + 1, cs] * buf_ref[off:off + rc, cs]
            o_ref[0, r * rc:(r + 1) * rc, cs] = _silu(acc + b_ref[:, cs])


def _ssmconv(u, w, bias, n_ctx):
    b, nt, _ = u.shape
    t = 256
    n_tiles = nt // t
    hb = t // 16
    last_hb = nt // 16 - 1
    cb = COL_XBC // SSM_XBC
    return pl.pallas_call(
        functools.partial(_ssmconv_kernel, t=t, n_ctx=n_ctx, n_tiles=n_tiles),
        out_shape=jax.ShapeDtypeStruct((b, nt, SSM_XBC), F32),
        grid=(b, n_tiles),
        in_specs=[pl.BlockSpec((1, t, SSM_XBC), lambda bi, i: (bi, i, cb)),
                  pl.BlockSpec((1, 16, SSM_XBC), lambda bi, i: (bi, jnp.maximum(i * hb - 1, 0), cb)),
                  pl.BlockSpec((1, 16, SSM_XBC), lambda bi, i: (bi, jnp.minimum((i + 1) * hb, last_hb), cb)),
                  pl.BlockSpec((SSM_CONV, SSM_XBC), lambda bi, i: (0, 0)),
                  pl.BlockSpec((1, SSM_XBC), lambda bi, i: (0, 0))],
        out_specs=pl.BlockSpec((1, t, SSM_XBC), lambda bi, i: (bi, i, 0)),
        scratch_shapes=[pltpu.VMEM((t + 32, SSM_XBC), F32)],
        compiler_params=_cparams(("parallel", "parallel")),
    )(u, u, u, w, bias.reshape(1, -1))


SCAN_CHUNKS = 2


def _chunk_index(s, reverse, n_chunks, n_ctx_chunks):
    if not reverse:
        return s
    return jnp.where(s < n_ctx_chunks, n_ctx_chunks - 1 - s, n_chunks + n_ctx_chunks - 1 - s)


def _softplus(v):
    return jnp.maximum(v, 0.0) + jnp.log(1.0 + jnp.exp(-jnp.abs(v)))


def _ssd_chunk(refs, rows, reverse):
    if reverse:
        xa_ref, dt_ref, dtb_ref, alog_ref, yf_ref, z_ref, dskip_ref, ng_ref, o_ref, h_ref = refs
    else:
        xa_ref, dt_ref, dtb_ref, alog_ref, o_ref, h_ref = refs
    L = CHUNK
    xa = xa_ref[0, rows, :]
    dt = _softplus(dt_ref[0, rows, :] + dtb_ref[...])
    la = dt * (-jnp.exp(alog_ref[...]))
    r = lax.broadcasted_iota(jnp.int32, (L, L), 0)
    c = lax.broadcasted_iota(jnp.int32, (L, L), 1)
    keep = (r <= c) if reverse else (r >= c)
    cum = jnp.dot(keep.astype(F32), la, preferred_element_type=F32, precision=HIGHEST)
    cum_t = cum.T
    off = SSM_HEADS if reverse else 0
    tot_row = 0 if reverse else L - 1
    hpg = SSM_HEADS // SSM_GROUPS
    gw = hpg * SSM_HEAD_DIM
    lane_head = lax.broadcasted_iota(jnp.int32, (L, gw), 1) // SSM_HEAD_DIM
    ys = []
    for g in range(SSM_GROUPS):
        b_f = xa[:, SSM_INNER + g * SSM_STATE:SSM_INNER + (g + 1) * SSM_STATE]
        c_b = xa[:, SSM_INNER + (SSM_GROUPS + g) * SSM_STATE:SSM_INNER + (SSM_GROUPS + g + 1) * SSM_STATE].astype(BF16)
        gram = lax.dot_general(c_b, b_f.astype(BF16), (((1,), (1,)), ((), ())), preferred_element_type=F32)
        b_t = b_f.T.astype(BF16)

        def expand(mat):
            out = jnp.broadcast_to(mat[:, off + g * hpg + hpg - 1:off + g * hpg + hpg], (L, gw))
            for j in range(hpg - 2, -1, -1):
                out = jnp.where(lane_head == j, mat[:, off + g * hpg + j:off + g * hpg + j + 1], out)
            return out

        dt_l = expand(dt)
        cum_l = expand(cum)
        tot_l = cum_l[tot_row:tot_row + 1, :]
        xdt = xa[:, g * gw:(g + 1) * gw] * dt_l
        xdt_b = xdt.astype(BF16)
        ms = []
        for j in range(hpg):
            hh = off + g * hpg + j
            seg = cum[:, hh:hh + 1] - cum_t[hh:hh + 1, :]
            dec = jnp.exp(jnp.where(keep, seg, -jnp.inf))
            ms.append((gram * dec).astype(BF16))
        rr = jnp.dot(jnp.concatenate(ms, axis=0), xdt_b, preferred_element_type=F32)
        y = jnp.where(lane_head == 0, rr[0:L], 0.0)
        for j in range(1, hpg):
            y = y + jnp.where(lane_head == j, rr[j * L:(j + 1) * L], 0.0)
        hg = h_ref[g]
        y = y + jnp.dot(c_b, hg.astype(BF16), preferred_element_type=F32) * jnp.exp(cum_l)
        wm = jnp.exp(tot_l - cum_l)
        h_ref[g] = jnp.exp(tot_l) * hg + jnp.dot(b_t, (xdt * wm).astype(BF16), preferred_element_type=F32)
        ys.append(y)
    y = jnp.concatenate(ys, axis=1)
    if not reverse:
        o_ref[0, rows, :] = y
        return
    y = yf_ref[0, rows, :] + y + dskip_ref[...] * xa[:, :SSM_INNER]
    y = y * _silu(z_ref[0, rows, :].astype(F32))
    outs = []
    for g in range(SSM_GROUPS):
        yy = y[:, g * gw:(g + 1) * gw]
        outs.append(yy * lax.rsqrt(jnp.mean(yy * yy, axis=-1, keepdims=True) + EPS) * ng_ref[:, g * gw:(g + 1) * gw])
    o_ref[0, rows, :] = jnp.concatenate(outs, axis=1).astype(o_ref.dtype)


def _ssd_specs(xa, u, dt_raw, dt_bias, a_log, yf, d_skip, norm_g, cidx, blk, reverse):
    pad16 = lambda p: jnp.pad(p.reshape(1, -1), ((0, 0), (0, LANES - 2 * SSM_HEADS)))
    in_specs = [pl.BlockSpec((1, blk, SSM_XBC), lambda bi, s: (bi, cidx(s), 0)),
                pl.BlockSpec((1, blk, LANES), lambda bi, s: (bi, cidx(s), 0)),
                pl.BlockSpec((1, LANES), lambda bi, s: (0, 0)),
                pl.BlockSpec((1, LANES), lambda bi, s: (0, 0))]
    args = [xa, dt_raw, pad16(dt_bias), pad16(a_log)]
    if reverse:
        in_specs += [pl.BlockSpec((1, blk, SSM_INNER), lambda bi, s: (bi, cidx(s), 0)),
                     pl.BlockSpec((1, blk, SSM_INNER), lambda bi, s: (bi, cidx(s), COL_Z // SSM_INNER)),
                     pl.BlockSpec((1, SSM_INNER), lambda bi, s: (0, 0)),
                     pl.BlockSpec((1, SSM_INNER), lambda bi, s: (0, 0))]
        args += [yf, u, jnp.repeat(d_skip, SSM_HEAD_DIM).reshape(1, -1), norm_g.reshape(1, -1)]
    return in_specs, args


def _ret_rope(v, cos, sin):
    lo, hi = v[:, :LANES], v[:, LANES:]
    return jnp.concatenate([lo * cos - hi * sin, lo * sin + hi * cos], axis=1)


def _ret_tables(refs, tab_ref, dc_ref, reverse):
    L = CHUNK
    lg = -jnp.exp(refs[5][...])
    t = lax.broadcasted_iota(jnp.int32, (L, 1), 0).astype(F32)
    if reverse:
        tab_ref[:, :RET_INNER] = jnp.exp(lg * (L - t))
        tab_ref[:, RET_INNER:] = jnp.exp(lg * t)
        return
    tab_ref[:, :RET_INNER] = jnp.exp(lg * (t + 1.0))
    tab_ref[:, RET_INNER:] = jnp.exp(lg * (L - 1.0 - t))
    lgb = -jnp.exp(refs[6][...])
    dl = (lax.broadcasted_iota(jnp.int32, (L, L), 0) - lax.broadcasted_iota(jnp.int32, (L, L), 1)).astype(F32)
    for h in range(RET_HEADS):
        vs = slice(h * RET_V_DIM, (h + 1) * RET_V_DIM)
        dc_ref[h] = jnp.where(dl > 0, jnp.exp(lg[:, vs] * jnp.maximum(dl, 0.0)),
                              jnp.where(dl < 0, jnp.exp(lgb[:, vs] * jnp.maximum(-dl, 0.0)), 2.0))


def _ret_chunk(refs, tab_ref, dc_ref, rows, reverse):
    if reverse:
        (q_ref, k_ref, v_ref, cos_ref, sin_ref, dec_ref, yf_ref, g_ref, gng_ref, gnb_ref, o_ref, h_ref) = refs
    else:
        (q_ref, k_ref, v_ref, cos_ref, sin_ref, dec_ref, decb_ref, o_ref, h_ref) = refs
    L = CHUNK
    qk = RET_HEADS * RET_QK_DIM
    cos, sin = cos_ref[rows, :], sin_ref[rows, :]
    q = _ret_rope(q_ref[0, rows, :].astype(F32), cos, sin)
    k = _ret_rope(k_ref[0, rows, :].astype(F32), cos, sin) * (RET_QK_DIM ** -0.5)
    v = v_ref[0, rows, :].astype(F32)
    q_b = q.astype(BF16)
    k_t = k.T.astype(BF16)
    lg = -jnp.exp(dec_ref[...])
    e_in = tab_ref[:, :RET_INNER]
    w_st = tab_ref[:, RET_INNER:]
    hs = h_ref[...]
    y = jnp.dot(q_b, hs.astype(BF16), preferred_element_type=F32) * e_in
    upd = jnp.dot(k_t, (v * w_st).astype(BF16), preferred_element_type=F32)
    row_head = (lax.broadcasted_iota(jnp.int32, (qk, RET_INNER), 0) % LANES) // (RET_QK_DIM // 2)
    lane_head = lax.broadcasted_iota(jnp.int32, (qk, RET_INNER), 1) // RET_V_DIM
    h_ref[...] = jnp.exp(lg * float(L)) * hs + jnp.where(row_head == lane_head, upd, 0.0)

    if not reverse:
        q_head = (lax.broadcasted_iota(jnp.int32, (L, qk), 1) % LANES) // (RET_QK_DIM // 2)
        qs = jnp.concatenate([jnp.where(q_head == h, q, 0.0) for h in range(RET_HEADS)], axis=0).astype(BF16)
        sc = jnp.dot(qs, k_t, preferred_element_type=F32)
        parts = []
        for h in range(RET_HEADS):
            vs = slice(h * RET_V_DIM, (h + 1) * RET_V_DIM)
            p = (sc[h * L:(h + 1) * L] * dc_ref[h]).astype(BF16)
            parts.append(jnp.dot(p, v[:, vs].astype(BF16), preferred_element_type=F32))
        o_ref[0, rows, :] = y + jnp.concatenate(parts, axis=1)
        return
    y = yf_ref[0, rows, :] + y
    outs = []
    for h in range(RET_HEADS):
        vs = slice(h * RET_V_DIM, (h + 1) * RET_V_DIM)
        yy = y[:, vs]
        mu = jnp.mean(yy, axis=-1, keepdims=True)
        var = jnp.mean(jnp.square(yy - mu), axis=-1, keepdims=True)
        outs.append((yy - mu) * lax.rsqrt(var + EPS) * gng_ref[:, vs] + gnb_ref[:, vs])
    o_ref[0, rows, :] = (_silu(g_ref[0, rows, :].astype(F32)) * jnp.concatenate(outs, axis=1)).astype(o_ref.dtype)


def _ret_specs(u, cos4, sin4, ret_decay, yf, gn_g, gn_b, cidx, blk, reverse):
    qk = RET_HEADS * RET_QK_DIM
    lane_dec = lambda d: jnp.repeat(d, RET_V_DIM).reshape(1, -1)
    in_specs = [pl.BlockSpec((1, blk, qk), lambda bi, s: (bi, cidx(s), COL_RQ // qk)),
                pl.BlockSpec((1, blk, qk), lambda bi, s: (bi, cidx(s), COL_RK // qk)),
                pl.BlockSpec((1, blk, RET_INNER), lambda bi, s: (bi, cidx(s), COL_RV // RET_INNER)),
                pl.BlockSpec((blk, LANES), lambda bi, s: (cidx(s), 0)),
                pl.BlockSpec((blk, LANES), lambda bi, s: (cidx(s), 0)),
                pl.BlockSpec((1, RET_INNER), lambda bi, s: (0, 0))]
    args = [u, u, u, cos4, sin4, lane_dec(ret_decay[1] if reverse else ret_decay[0])]
    if reverse:
        in_specs += [pl.BlockSpec((1, blk, RET_INNER), lambda bi, s: (bi, cidx(s), 0)),
                     pl.BlockSpec((1, blk, RET_INNER), lambda bi, s: (bi, cidx(s), COL_RG // RET_INNER)),
                     pl.BlockSpec((1, RET_INNER), lambda bi, s: (0, 0)),
                     pl.BlockSpec((1, RET_INNER), lambda bi, s: (0, 0))]
        args += [yf, u, gn_g.reshape(1, -1), gn_b.reshape(1, -1)]
    else:
        in_specs += [pl.BlockSpec((1, RET_INNER), lambda bi, s: (0, 0))]
        args += [lane_dec(ret_decay[1])]
    return in_specs, args


def _scan_kernel(*refs, n_ssd_in, reverse):
    n_in = len(refs) - 6
    ssd_o, ret_o, ssd_h, ret_h, ret_tab, ret_dc = refs[n_in:]
    ssd_refs = refs[:n_ssd_in] + (ssd_o, ssd_h)
    ret_refs = refs[n_ssd_in:n_in] + (ret_o, ret_h)

    @pl.when(pl.program_id(1) == 0)
    def _():
        ssd_h[...] = jnp.zeros_like(ssd_h)
        ret_h[...] = jnp.zeros_like(ret_h)
        _ret_tables(ret_refs, ret_tab, ret_dc, reverse)

    order = range(SCAN_CHUNKS - 1, -1, -1) if reverse else range(SCAN_CHUNKS)
    for ci in order:
        rows = slice(ci * CHUNK, (ci + 1) * CHUNK)
        _ssd_chunk(ssd_refs, rows, reverse)
        _ret_chunk(ret_refs, ret_tab, ret_dc, rows, reverse)


def _scan(ssd_in, ret_in, n_ctx, reverse):
    b, nt, _ = ssd_in[0].shape
    blk = SCAN_CHUNKS * CHUNK
    cidx = functools.partial(_chunk_index, reverse=reverse, n_chunks=nt // blk, n_ctx_chunks=n_ctx // blk)
    ssd_specs, ssd_args = _ssd_specs(*ssd_in, cidx, blk, reverse)
    ret_specs, ret_args = _ret_specs(*ret_in, cidx, blk, reverse)
    out_spec = lambda w: pl.BlockSpec((1, blk, w), lambda bi, s: (bi, cidx(s), 0))
    out_dtype = BF16 if reverse else F32
    return pl.pallas_call(
        functools.partial(_scan_kernel, n_ssd_in=len(ssd_args), reverse=reverse),
        out_shape=(jax.ShapeDtypeStruct((b, nt, SSM_INNER), out_dtype), jax.ShapeDtypeStruct((b, nt, RET_INNER), out_dtype)),
        grid=(b, nt // blk),
        in_specs=ssd_specs + ret_specs,
        out_specs=(out_spec(SSM_INNER), out_spec(RET_INNER)),
        scratch_shapes=[pltpu.VMEM((SSM_GROUPS, SSM_STATE, SSM_INNER // SSM_GROUPS), F32),
                        pltpu.VMEM((RET_HEADS * RET_QK_DIM, RET_INNER), F32),
                        pltpu.VMEM((CHUNK, 2 * RET_INNER), F32),
                        pltpu.VMEM((RET_HEADS, CHUNK, CHUNK), F32)],
        compiler_params=_cparams(("parallel", "arbitrary")),
    )(*ssd_args, *ret_args)


def _mla_rope(v, c, s1, s2):
    return v * c + pltpu.roll(v, 16, 1) * s1 + pltpu.roll(v, LANES - 16, 1) * s2


def _rms(v, g):
    return v * lax.rsqrt(jnp.mean(v * v, axis=-1, keepdims=True) + EPS) * g


def _mlaprep_kernel(cq_ref, ckv_ref, kr_ref, qg_ref, kvg_ref, wq_ref, wk_ref, wv_ref, c_ref, s1_ref, s2_ref,
                    qt_out, k_out, vt_out):
    c, s1, s2 = c_ref[...], s1_ref[...], s2_ref[...]
    scale = (MLA_NOPE + MLA_ROPE) ** -0.5 * math.log2(math.e)
    q = jnp.dot(_rms(cq_ref[0].astype(F32), qg_ref[...]).astype(BF16), wq_ref[...], preferred_element_type=F32)
    ckv = _rms(ckv_ref[0].astype(F32), kvg_ref[...]).astype(BF16)
    kn = jnp.dot(ckv, wk_ref[...], preferred_element_type=F32)
    vt_out[0] = jnp.dot(ckv, wv_ref[...], preferred_element_type=F32).T.astype(BF16)
    krr = _mla_rope(kr_ref[0].astype(F32), c, s1, s2)
    for h in range(MLA_HEADS):
        hs = slice(h * LANES, (h + 1) * LANES)
        qt_out[0, hs, :] = (_mla_rope(q[:, hs], c, s1, s2) * scale).T.astype(BF16)
        k_out[0, :, hs] = (kn[:, hs] + krr).astype(BF16)


def _mlaprep(u, q_g, kv_g, wq, wk, wv, ctab, s1tab, s2tab):
    b, nt, _ = u.shape
    tm = _tile(nt, 1280, 256)
    hw = MLA_HEADS * LANES
    const = lambda shape: pl.BlockSpec(shape, lambda bi, i: (0, 0))
    return pl.pallas_call(
        _mlaprep_kernel,
        out_shape=(jax.ShapeDtypeStruct((b, hw, nt), BF16), jax.ShapeDtypeStruct((b, nt, hw), BF16),
                   jax.ShapeDtypeStruct((b, MLA_HEADS * MLA_V, nt), BF16)),
        grid=(b, nt // tm),
        in_specs=[pl.BlockSpec((1, tm, MLA_Q_RANK), lambda bi, i: (bi, i, COL_CQ // MLA_Q_RANK)),
                  pl.BlockSpec((1, tm, MLA_KV_RANK), lambda bi, i: (bi, i, COL_CKV // MLA_KV_RANK)),
                  pl.BlockSpec((1, tm, LANES), lambda bi, i: (bi, i, COL_KR // LANES)),
                  const((1, MLA_Q_RANK)), const((1, MLA_KV_RANK)),
                  const((MLA_Q_RANK, hw)), const((MLA_KV_RANK, hw)), const((MLA_KV_RANK, MLA_HEADS * MLA_V)),
                  pl.BlockSpec((tm, LANES), lambda bi, i: (i, 0)),
                  pl.BlockSpec((tm, LANES), lambda bi, i: (i, 0)),
                  pl.BlockSpec((tm, LANES), lambda bi, i: (i, 0))],
        out_specs=(pl.BlockSpec((1, hw, tm), lambda bi, i: (bi, 0, i)),
                   pl.BlockSpec((1, tm, hw), lambda bi, i: (bi, i, 0)),
                   pl.BlockSpec((1, MLA_HEADS * MLA_V, tm), lambda bi, i: (bi, 0, i))),
        compiler_params=_cparams(("parallel", "parallel")),
    )(u, u, u, q_g.reshape(1, -1), kv_g.reshape(1, -1), wq, wk, wv, ctab, s1tab, s2tab)


ACC_ROWS = MLA_V + 16


def _flash_kernel(k_ref, qt_ref, vt_ref, o_ref, m_ref, off_ref, acc_ref, *, tq, tk, n_ctx, nk):
    i = pl.program_id(2)
    j = pl.program_id(3)

    @pl.when(j == 0)
    def _():
        m_ref[...] = jnp.full_like(m_ref, -jnp.inf)
        acc_ref[...] = jnp.zeros_like(acc_ref)
        for hh in range(2):
            hs = slice(hh * LANES, (hh + 1) * LANES)
            s0 = jnp.dot(k_ref[0, :LANES, hs], qt_ref[0, hs, :], preferred_element_type=F32)
            off_ref[hh] = jnp.max(s0, axis=0, keepdims=True)

    def step(masked):
        ones = jnp.ones((ACC_ROWS - MLA_V, tk), BF16)
        for hh in range(2):
            hs = slice(hh * LANES, (hh + 1) * LANES)
            s = jnp.dot(k_ref[0, :, hs], qt_ref[0, hs, :], preferred_element_type=F32)
            m_prev = m_ref[hh]
            off = jnp.where(m_prev == -jnp.inf, off_ref[hh], m_prev)
            s = (s - off).astype(BF16)
            if masked:
                keys = j * tk + lax.broadcasted_iota(jnp.int32, (tk, n_ctx), 0)
                s_ctx = jnp.where(keys >= n_ctx, -3e38, s[:, :n_ctx])
                s = s_ctx if tq == n_ctx else jnp.concatenate([s_ctx, s[:, n_ctx:]], axis=1)
            rel = jnp.maximum(jnp.max(s, axis=0, keepdims=True).astype(F32), m_prev - off)
            m_new = rel + off
            alpha = jnp.exp2(m_prev - m_new)
            p = jnp.exp2(s - rel.astype(BF16))
            vs = slice(hh * MLA_V, (hh + 1) * MLA_V)
            lhs = jnp.concatenate([vt_ref[0, vs, :], ones], axis=0)
            acc_ref[hh] = alpha * acc_ref[hh] + jnp.dot(lhs, p, preferred_element_type=F32)
            m_ref[hh] = m_new

    @pl.when(i * tq < n_ctx)
    def _():
        step(True)

    @pl.when(i * tq >= n_ctx)
    def _():
        step(False)

    @pl.when(j == nk - 1)
    def _():
        outs = []
        for hh in range(2):
            a = acc_ref[hh]
            outs.append(a[:MLA_V] * (1.0 / a[MLA_V:MLA_V + 1]))
        o_ref[0] = jnp.concatenate(outs, axis=0).T.astype(o_ref.dtype)


def _flash(k, qt, vt, n_ctx, tq_target=3328, tk_target=1280):
    b, nt, _ = k.shape
    tq = _tile(nt, tq_target, LANES)
    tk = _tile(nt, tk_target, LANES)
    assert tq >= n_ctx and n_ctx % LANES == 0
    nk = nt // tk
    pairs = MLA_HEADS // 2
    return pl.pallas_call(
        functools.partial(_flash_kernel, tq=tq, tk=tk, n_ctx=n_ctx, nk=nk),
        out_shape=jax.ShapeDtypeStruct((b, nt, MLA_HEADS * MLA_V), BF16),
        grid=(b, pairs, nt // tq, nk),
        in_specs=[pl.BlockSpec((1, tk, 2 * LANES), lambda bi, p, i, j: (bi, j, p)),
                  pl.BlockSpec((1, 2 * LANES, tq), lambda bi, p, i, j: (bi, p, i)),
                  pl.BlockSpec((1, 2 * MLA_V, tk), lambda bi, p, i, j: (bi, p, j))],
        out_specs=pl.BlockSpec((1, tq, 2 * MLA_V), lambda bi, p, i, j: (bi, i, p)),
        scratch_shapes=[pltpu.VMEM((2, 1, tq), F32), pltpu.VMEM((2, 1, tq), F32), pltpu.VMEM((2, ACC_ROWS, tq), F32)],
        compiler_params=_cparams(("parallel", "parallel", "parallel", "arbitrary")),
    )(k, qt, vt)


def _merge_kernel(xs_ref, gl_ref, b0_ref, b1_ref, b2_ref, b3_ref, wb_ref, wo_ref, mod_ref, o_ref, *, tm, n_ctx):
    i = pl.program_id(1)
    merged = None
    for n, br in enumerate((b0_ref, b1_ref, b2_ref, b3_ref)):
        proj = jnp.dot(br[0], wb_ref[n], preferred_element_type=F32)
        term = jax.nn.sigmoid(gl_ref[0, :, n * D_MODEL:(n + 1) * D_MODEL].astype(F32)) * proj
        merged = term if merged is None else merged + term
    out = jnp.dot(merged.astype(BF16), wo_ref[...], preferred_element_type=F32)
    o_ref[0] = xs_ref[0] + _res_gate(mod_ref[0], i * tm, tm, n_ctx, 0) * out


def _merge(xs, u, branches, wb, wo, modtab, n_ctx):
    b, nt, d = xs.shape
    tm = 256
    row = lambda w: pl.BlockSpec((1, tm, w), lambda bi, i: (bi, i, 0))
    return pl.pallas_call(
        functools.partial(_merge_kernel, tm=tm, n_ctx=n_ctx),
        out_shape=jax.ShapeDtypeStruct((b, nt, d), F32),
        grid=(b, nt // tm),
        in_specs=[row(d), row(N_BRANCH * d), row(BRANCH_DIM), row(BRANCH_DIM), row(BRANCH_DIM), row(BRANCH_DIM),
                  pl.BlockSpec((N_BRANCH, BRANCH_DIM, d), lambda bi, i: (0, 0, 0)),
                  pl.BlockSpec((d, d), lambda bi, i: (0, 0)),
                  pl.BlockSpec((1, 16, d), lambda bi, i: (bi, 0, 0))],
        out_specs=row(d),
        compiler_params=_cparams(("parallel", "parallel")),
    )(xs, u, *branches, wb, wo, modtab)


def _ffn_kernel(xs_ref, mod_ref, g_ref, wa_ref, wg_ref, wo_ref, o_ref, h_ref, acc_ref, *, tm, n_ctx, nj):
    i = pl.program_id(1)
    j = pl.program_id(2)

    @pl.when(j == 0)
    def _():
        h_ref[...] = _norm_mod(xs_ref[0], mod_ref[0], g_ref[...], i * tm, n_ctx, 1).astype(BF16)

    h = h_ref[...]
    a = jnp.dot(h, wa_ref[...], preferred_element_type=F32)
    gate = jnp.dot(h, wg_ref[...], preferred_element_type=F32)
    part = jnp.dot((_silu(gate) * a).astype(BF16), wo_ref[...], preferred_element_type=F32)

    @pl.when(j == 0)
    def _():
        acc_ref[...] = part

    @pl.when(j > 0)
    def _():
        acc_ref[...] += part

    @pl.when(j == nj - 1)
    def _():
        o_ref[0] = xs_ref[0] + _res_gate(mod_ref[0], i * tm, tm, n_ctx, 1) * acc_ref[...]


def _ffn(xs, modtab, g, w_in, w_out, n_ctx):
    b, nt, d = xs.shape
    f = w_out.shape[0]
    tm = _tile(nt, 640, 128)
    tf = _tile(f, 1408, LANES)
    nj = f // tf
    return pl.pallas_call(
        functools.partial(_ffn_kernel, tm=tm, n_ctx=n_ctx, nj=nj),
        out_shape=jax.ShapeDtypeStruct((b, nt, d), F32),
        grid=(b, nt // tm, nj),
        in_specs=[pl.BlockSpec((1, tm, d), lambda bi, i, j: (bi, i, 0)),
                  pl.BlockSpec((1, 16, d), lambda bi, i, j: (bi, 0, 0)),
                  pl.BlockSpec((1, d), lambda bi, i, j: (0, 0)),
                  pl.BlockSpec((d, tf), lambda bi, i, j: (0, j)),
                  pl.BlockSpec((d, tf), lambda bi, i, j: (0, nj + j)),
                  pl.BlockSpec((tf, d), lambda bi, i, j: (j, 0))],
        out_specs=pl.BlockSpec((1, tm, d), lambda bi, i, j: (bi, i, 0)),
        scratch_shapes=[pltpu.VMEM((tm, d), BF16), pltpu.VMEM((tm, d), F32)],
        compiler_params=_cparams(("parallel", "parallel", "arbitrary")),
    )(xs, modtab, g.reshape(1, d), w_in, w_in, w_out)


def _final_kernel(x_ref, g_ref, o_ref):
    o_ref[0] = _rms(x_ref[0], g_ref[...])


def _final_norm(xs, g, n_ctx):
    b, nt, d = xs.shape
    t = 256
    skip = n_ctx // t
    return pl.pallas_call(
        _final_kernel,
        out_shape=jax.ShapeDtypeStruct((b, nt - n_ctx, d), F32),
        grid=(b, (nt - n_ctx) // t),
        in_specs=[pl.BlockSpec((1, t, d), lambda bi, i: (bi, i + skip, 0)),
                  pl.BlockSpec((1, d), lambda bi, i: (0, 0))],
        out_specs=pl.BlockSpec((1, t, d), lambda bi, i: (bi, i, 0)),
        compiler_params=_cparams(("parallel", "parallel")),
    )(xs, g.reshape(1, d))


def _prep_w_in(w):
    d = w.shape[0]
    o_conv, o_z, o_xbc, o_dt = 0, 1024, 1536, 2560
    o_rq, o_rk, o_rv, o_rg = 2576, 2832, 3088, 3600
    o_cq, o_ckv, o_kr, o_gl = 4112, 4496, 4752, 4784
    seg = lambda o, n: w[:, o:o + n]

    def halves(o):
        s = seg(o, RET_HEADS * RET_QK_DIM).reshape(d, RET_HEADS, 2, RET_QK_DIM // 2)
        return jnp.transpose(s, (0, 2, 1, 3)).reshape(d, RET_HEADS * RET_QK_DIM)

    zeros = lambda n: jnp.zeros((d, n), w.dtype)
    parts = [seg(o_gl, N_BRANCH * D_MODEL), seg(o_conv, 2 * CONV_DIM), seg(o_xbc, SSM_XBC), seg(o_z, SSM_INNER),
             seg(o_rv, RET_INNER), seg(o_rg, RET_INNER), halves(o_rq), halves(o_rk),
             seg(o_ckv, MLA_KV_RANK), seg(o_cq, MLA_Q_RANK),
             seg(o_dt, 2 * SSM_HEADS), zeros(LANES - 2 * SSM_HEADS),
             zeros(MLA_NOPE), seg(o_kr, MLA_ROPE), zeros(LANES - MLA_NOPE - MLA_ROPE),
             zeros(LANES)]
    out = jnp.concatenate(parts, axis=1).astype(BF16)
    assert out.shape[1] == N_IN_PAD
    return out


def _prep_mla_w(w_uq, w_ukv):
    rq, rkv = w_uq.shape[0], w_ukv.shape[0]
    hd = MLA_NOPE + MLA_ROPE
    wq = jnp.pad(w_uq.reshape(rq, MLA_HEADS, hd), ((0, 0), (0, 0), (0, LANES - hd))).reshape(rq, MLA_HEADS * LANES)
    kv = w_ukv.reshape(rkv, MLA_HEADS, MLA_NOPE + MLA_V)
    wk = jnp.pad(kv[:, :, :MLA_NOPE], ((0, 0), (0, 0), (0, LANES - MLA_NOPE))).reshape(rkv, MLA_HEADS * LANES)
    wv = kv[:, :, MLA_NOPE:].reshape(rkv, MLA_HEADS * MLA_V)
    return wq.astype(BF16), wk.astype(BF16), wv.astype(BF16)


def _axial_angles(n, rot_dim):
    rows = n // GRID_W
    row = jnp.repeat(jnp.arange(rows, dtype=F32), GRID_W)
    col = jnp.tile(jnp.arange(GRID_W, dtype=F32), rows)
    nf = rot_dim // 4
    inv = ROPE_BASE ** (-jnp.arange(nf, dtype=F32) / nf)
    return jnp.concatenate([row[:, None] * inv, col[:, None] * inv], axis=-1)


def _rope_tables(n_lat, n_ctx):
    ang = _axial_angles(n_lat, RET_QK_DIM)
    cos4 = jnp.tile(jnp.cos(ang), (1, RET_HEADS))
    sin4 = jnp.tile(jnp.sin(ang), (1, RET_HEADS))
    cos4 = jnp.concatenate([jnp.ones((n_ctx, LANES), F32), cos4], axis=0)
    sin4 = jnp.concatenate([jnp.zeros((n_ctx, LANES), F32), sin4], axis=0)
    ang = _axial_angles(n_lat, MLA_ROPE)
    c, s = jnp.cos(ang), jnp.sin(ang)
    half = MLA_ROPE // 2
    one = jnp.ones((n_lat, MLA_NOPE), F32)
    zero = lambda w: jnp.zeros((n_lat, w), F32)
    tail = LANES - MLA_NOPE - MLA_ROPE
    ctab = jnp.concatenate([one, c, c, jnp.ones((n_lat, tail), F32)], axis=1)
    s1 = jnp.concatenate([zero(MLA_NOPE + half), s, zero(tail)], axis=1)
    s2 = jnp.concatenate([zero(MLA_NOPE), -s, zero(half + tail)], axis=1)
    ctab = jnp.concatenate([jnp.ones((n_ctx, LANES), F32), ctab], axis=0)
    s1 = jnp.concatenate([jnp.zeros((n_ctx, LANES), F32), s1], axis=0)
    s2 = jnp.concatenate([jnp.zeros((n_ctx, LANES), F32), s2], axis=0)
    return cos4, sin4, ctab, s1, s2


def kernel(x, c, ctx, c_ctx, w_ada, b_ada, norm1_g, norm2_g, w_in, conv_w, conv_b, conv_ln_g, conv_ln_b,
           ssm_conv_w, ssm_conv_b, ssm_dt_bias, ssm_a_log, ssm_d, ssm_norm_g, ret_decay, ret_gn_g, ret_gn_b,
           mla_q_norm_g, mla_kv_norm_g, mla_w_uq, mla_w_ukv, w_branch, w_out, w_ffn_in, w_ffn_out, final_norm_g):
    batch, n_lat, d = x.shape
    n_ctx = ctx.shape[1]
    depth = w_in.shape[0]
    assert d == D_MODEL and n_ctx % 256 == 0 and n_lat % 256 == 0 and batch + 1 <= 8
    assert n_ctx % (SCAN_CHUNKS * CHUNK) == 0
    cos4, sin4, ctab, s1tab, s2tab = _rope_tables(n_lat, n_ctx)
    xs = jnp.concatenate([ctx, x], axis=1)
    cvec = jnp.zeros((8, d), F32).at[:batch].set(c).at[batch].set(c_ctx)
    for i in range(depth):
        mod = _ada(cvec, w_ada[i], b_ada[i]).reshape(8, 6, d)
        modtab = jnp.zeros((batch, 16, d), F32)
        modtab = modtab.at[:, 0:6].set(jnp.broadcast_to(mod[batch], (batch, 6, d))).at[:, 8:14].set(mod[:batch])
        u, dt_raw = _inproj(xs, modtab, norm1_g[i], _prep_w_in(w_in[i]), n_ctx)
        conv_y = _convmod(u, conv_w[i], conv_b[i], conv_ln_g[i], conv_ln_b[i], n_ctx)
        xa = _ssmconv(u, ssm_conv_w[i], ssm_conv_b[i], n_ctx)
        ssm_f, ret_f = _scan((xa, u, dt_raw, ssm_dt_bias[i], ssm_a_log[i], None, None, None),
                             (u, cos4, sin4, ret_decay[i], None, None, None), n_ctx, False)
        ssm_y, ret_y = _scan((xa, u, dt_raw, ssm_dt_bias[i], ssm_a_log[i], ssm_f, ssm_d[i], ssm_norm_g[i]),
                             (u, cos4, sin4, ret_decay[i], ret_f, ret_gn_g[i], ret_gn_b[i]), n_ctx, True)
        wq, wk, wv = _prep_mla_w(mla_w_uq[i], mla_w_ukv[i])
        qt, k, vt = _mlaprep(u, mla_q_norm_g[i], mla_kv_norm_g[i], wq, wk, wv, ctab, s1tab, s2tab)
        att = _flash(k, qt, vt, n_ctx)
        xs = _merge(xs, u, (conv_y, ssm_y, ret_y, att), w_branch[i].astype(BF16), w_out[i].astype(BF16),
                    modtab, n_ctx)
        xs = _ffn(xs, modtab, norm2_g[i], w_ffn_in[i].astype(BF16), w_ffn_out[i].astype(BF16), n_ctx)
    return _final_norm(xs, final_norm_g, n_ctx)
```

```python
import functools
import math

import jax
import jax.numpy as jnp
from jax import lax
from jax.experimental import pallas as pl
from jax.experimental.pallas import tpu as pltpu

F32 = jnp.float32
BF16 = jnp.bfloat16
HIGHEST = lax.Precision.HIGHEST

D_MODEL = 1024
GRID_W = 64
CHUNK = 128
ROPE_BASE = 10000.0
EPS = 1e-6
BRANCH_DIM = D_MODEL // 2
N_BRANCH = 4
CONV_DIM = BRANCH_DIM
CONV_WIDTH = 31
SSM_INNER = BRANCH_DIM
SSM_HEAD_DIM = 64
SSM_HEADS = SSM_INNER // SSM_HEAD_DIM
SSM_GROUPS = 2
SSM_STATE = 128
SSM_CONV = 5
SSM_XBC = SSM_INNER + 2 * SSM_GROUPS * SSM_STATE
RET_HEADS = 4
RET_QK_DIM = 64
RET_INNER = BRANCH_DIM
RET_V_DIM = RET_INNER // RET_HEADS
MLA_HEADS = 8
MLA_NOPE = 64
MLA_ROPE = 32
MLA_V = BRANCH_DIM // MLA_HEADS
MLA_Q_RANK = 384
MLA_KV_RANK = 256
FFN_DIM = ((8 * D_MODEL // 3 + 255) // 256) * 256

LANES = 128
SUBLANES = 8
VMEM_LIMIT_BYTES = 56 * 1024 * 1024

COL_GL = 0
COL_CONV = 4096
COL_XBC = 5120
COL_Z = 6144
COL_RV = 6656
COL_RG = 7168
COL_RQ = 7680
COL_RK = 7936
COL_CKV = 8192
COL_CQ = 8448
COL_DT = 8832
COL_KR = 8960
N_IN_PAD = 9216


def _tile(n, target, mult):
    best = None
    for t in range(mult, min(n, target) + 1, mult):
        if n % t == 0:
            best = t
    assert best is not None, (n, target, mult)
    return best


def _cparams(sem):
    return pltpu.CompilerParams(dimension_semantics=sem, vmem_limit_bytes=VMEM_LIMIT_BYTES)


def _silu(v):
    return v * jax.nn.sigmoid(v)


def _norm_mod(x, mod, g, row0, n_ctx, k):
    y = x * lax.rsqrt(jnp.mean(x * x, axis=-1, keepdims=True) + EPS) * g
    rows = row0 + lax.broadcasted_iota(jnp.int32, (x.shape[0], 1), 0)
    is_ctx = rows < n_ctx
    shift = jnp.where(is_ctx, mod[3 * k:3 * k + 1], mod[8 + 3 * k:9 + 3 * k])
    scale = jnp.where(is_ctx, mod[3 * k + 1:3 * k + 2], mod[9 + 3 * k:10 + 3 * k])
    return y * (1.0 + scale) + shift


def _res_gate(mod, row0, n_rows, n_ctx, k):
    rows = row0 + lax.broadcasted_iota(jnp.int32, (n_rows, 1), 0)
    return jnp.where(rows < n_ctx, mod[3 * k + 2:3 * k + 3], mod[10 + 3 * k:11 + 3 * k])


def _ada_kernel(c_ref, w_ref, b_ref, o_ref):
    o_ref[...] = jnp.dot(_silu(c_ref[...]), w_ref[...], preferred_element_type=F32, precision=HIGHEST) + b_ref[...]


def _ada(cvec, w, b):
    n = w.shape[1]
    tn = _tile(n, 1536, LANES)
    return pl.pallas_call(
        _ada_kernel,
        out_shape=jax.ShapeDtypeStruct((cvec.shape[0], n), F32),
        grid=(n // tn,),
        in_specs=[pl.BlockSpec(cvec.shape, lambda j: (0, 0)),
                  pl.BlockSpec((w.shape[0], tn), lambda j: (0, j)),
                  pl.BlockSpec((1, tn), lambda j: (0, j))],
        out_specs=pl.BlockSpec((cvec.shape[0], tn), lambda j: (0, j)),
        compiler_params=_cparams(("arbitrary",)),
    )(cvec, w, b.reshape(1, n))


def _inproj_kernel(x_ref, mod_ref, g_ref, w_ref, o_ref, dt_ref, h_ref, *, tm, tn, n_ctx):
    i = pl.program_id(1)
    j = pl.program_id(2)

    @pl.when(j == 0)
    def _():
        h_ref[...] = _norm_mod(x_ref[0], mod_ref[0], g_ref[...], i * tm, n_ctx, 0).astype(BF16)

    res = jnp.dot(h_ref[...], w_ref[...], preferred_element_type=F32)
    o_ref[0] = res.astype(BF16)

    @pl.when(j == COL_DT // tn)
    def _():
        dt_ref[0] = res[:, COL_DT % tn:COL_DT % tn + LANES]


def _inproj(xs, modtab, g, w, n_ctx):
    b, nt, d = xs.shape
    n = w.shape[1]
    tm = _tile(nt, 1664, LANES)
    tn = _tile(n, 1536, LANES)
    assert COL_DT % tn + LANES <= tn
    return pl.pallas_call(
        functools.partial(_inproj_kernel, tm=tm, tn=tn, n_ctx=n_ctx),
        out_shape=(jax.ShapeDtypeStruct((b, nt, n), BF16), jax.ShapeDtypeStruct((b, nt, LANES), F32)),
        grid=(b, nt // tm, n // tn),
        in_specs=[pl.BlockSpec((1, tm, d), lambda bi, i, j: (bi, i, 0)),
                  pl.BlockSpec((1, 16, d), lambda bi, i, j: (bi, 0, 0)),
                  pl.BlockSpec((1, d), lambda bi, i, j: (0, 0)),
                  pl.BlockSpec((d, tn), lambda bi, i, j: (0, j))],
        out_specs=(pl.BlockSpec((1, tm, tn), lambda bi, i, j: (bi, i, j)),
                   pl.BlockSpec((1, tm, LANES), lambda bi, i, j: (bi, i, 0))),
        scratch_shapes=[pltpu.VMEM((tm, d), BF16)],
        compiler_params=_cparams(("parallel", "parallel", "arbitrary")),
    )(xs, modtab, g.reshape(1, d), w)


def _halo_flags(i, t, n_ctx, n_tiles):
    zero_prev = jnp.logical_or(i == 0, i * t == n_ctx)
    zero_next = jnp.logical_or((i + 1) * t == n_ctx, i == n_tiles - 1)
    return zero_prev, zero_next


def _convmod_kernel(cur_ref, prev_ref, next_ref, w_ref, b_ref, lg_ref, lb_ref, o_ref, buf_ref,
                    *, t, n_ctx, n_tiles):
    i = pl.program_id(1)
    zero_prev, zero_next = _halo_flags(i, t, n_ctx, n_tiles)

    def glu(u):
        u = u.astype(F32)
        return u[:, :CONV_DIM] * jax.nn.sigmoid(u[:, CONV_DIM:])

    halo = 16
    buf_ref[0, 0:halo, :] = jnp.where(zero_prev, 0.0, glu(prev_ref[0]))
    buf_ref[0, halo:halo + t, :] = glu(cur_ref[0])
    buf_ref[0, halo + t:2 * halo + t, :] = jnp.where(zero_next, 0.0, glu(next_ref[0]))
    n_sh = t + 2 * halo - SUBLANES
    for sh in range(1, SUBLANES):
        buf_ref[sh, 0:n_sh, :] = buf_ref[0, sh:sh + n_sh, :]
    pad = CONV_WIDTH // 2
    h = jnp.zeros((t, CONV_DIM), F32)
    for k in range(CONV_WIDTH):
        off = halo + k - pad
        base = off - off % SUBLANES
        h = h + w_ref[k:k + 1, :] * buf_ref[off % SUBLANES, base:base + t, :]
    h = h + b_ref[...]
    mu = jnp.mean(h, axis=-1, keepdims=True)
    var = jnp.mean(jnp.square(h - mu), axis=-1, keepdims=True)
    y = (h - mu) * lax.rsqrt(var + EPS) * lg_ref[...] + lb_ref[...]
    o_ref[0] = _silu(y).astype(o_ref.dtype)


def _convmod(u, w, bias, ln_g, ln_b, n_ctx):
    b, nt, _ = u.shape
    t = 256
    n_tiles = nt // t
    hb = t // 16
    last_hb = nt // 16 - 1
    width = 2 * CONV_DIM
    cb = COL_CONV // width
    return pl.pallas_call(
        functools.partial(_convmod_kernel, t=t, n_ctx=n_ctx, n_tiles=n_tiles),
        out_shape=jax.ShapeDtypeStruct((b, nt, CONV_DIM), BF16),
        grid=(b, n_tiles),
        in_specs=[pl.BlockSpec((1, t, width), lambda bi, i: (bi, i, cb)),
                  pl.BlockSpec((1, 16, width), lambda bi, i: (bi, jnp.maximum(i * hb - 1, 0), cb)),
                  pl.BlockSpec((1, 16, width), lambda bi, i: (bi, jnp.minimum((i + 1) * hb, last_hb), cb)),
                  pl.BlockSpec((CONV_WIDTH, CONV_DIM), lambda bi, i: (0, 0)),
                  pl.BlockSpec((1, CONV_DIM), lambda bi, i: (0, 0)),
                  pl.BlockSpec((1, CONV_DIM), lambda bi, i: (0, 0)),
                  pl.BlockSpec((1, CONV_DIM), lambda bi, i: (0, 0))],
        out_specs=pl.BlockSpec((1, t, CONV_DIM), lambda bi, i: (bi, i, 0)),
        scratch_shapes=[pltpu.VMEM((SUBLANES, t + 32, CONV_DIM), F32)],
        compiler_params=_cparams(("parallel", "parallel")),
    )(u, u, u, w, bias.reshape(1, -1), ln_g.reshape(1, -1), ln_b.reshape(1, -1))


def _ssmconv_kernel(cur_ref, prev_ref, next_ref, w_ref, b_ref, o_ref, buf_ref, *, t, n_ctx, n_tiles):
    i = pl.program_id(1)
    zero_prev, zero_next = _halo_flags(i, t, n_ctx, n_tiles)
    halo = 16
    buf_ref[0:halo, :] = jnp.where(zero_prev, 0.0, prev_ref[0].astype(F32))
    buf_ref[halo:halo + t, :] = cur_ref[0].astype(F32)
    buf_ref[halo + t:2 * halo + t, :] = jnp.where(zero_next, 0.0, next_ref[0].astype(F32))
    pad = SSM_CONV // 2
    rc = 64
    for r in range(t // rc):
        for c in range(SSM_XBC // LANES):
            cs = slice(c * LANES, (c + 1) * LANES)
            acc = jnp.zeros((rc, LANES), F32)
            for k in range(SSM_CONV):
                off = halo + r * rc + k - pad
                acc = acc + w_ref[k:k + 1, cs] * buf_ref[off:off + rc, cs]
            o_ref[0, r * rc:(r + 1) * rc, cs] = _silu(acc + b_ref[:, cs])


def _ssmconv(u, w, bias, n_ctx):
    b, nt, _ = u.shape
    t = 256
    n_tiles = nt // t
    hb = t // 16
    last_hb = nt // 16 - 1
    cb = COL_XBC // SSM_XBC
    return pl.pallas_call(
        functools.partial(_ssmconv_kernel, t=t, n_ctx=n_ctx, n_tiles=n_tiles),
        out_shape=jax.ShapeDtypeStruct((b, nt, SSM_XBC), F32),
        grid=(b, n_tiles),
        in_specs=[pl.BlockSpec((1, t, SSM_XBC), lambda bi, i: (bi, i, cb)),
                  pl.BlockSpec((1, 16, SSM_XBC), lambda bi, i: (bi, jnp.maximum(i * hb - 1, 0), cb)),
                  pl.BlockSpec((1, 16, SSM_XBC), lambda bi, i: (bi, jnp.minimum((i + 1) * hb, last_hb), cb)),
                  pl.BlockSpec((SSM_CONV, SSM_XBC), lambda bi, i: (0, 0)),
                  pl.BlockSpec((1, SSM_XBC), lambda bi, i: (0, 0))],
        out_specs=pl.BlockSpec((1, t, SSM_XBC), lambda bi, i: (bi, i, 0)),
        scratch_shapes=[pltpu.VMEM((t + 32, SSM_XBC), F32)],
        compiler_params=_cparams(("parallel", "parallel")),
    )(u, u, u, w, bias.reshape(1, -1))


SCAN_CHUNKS = 2


def _chunk_index(s, reverse, n_chunks, n_ctx_chunks):
    if not reverse:
        return s
    return jnp.where(s < n_ctx_chunks, n_ctx_chunks - 1 - s, n_chunks + n_ctx_chunks - 1 - s)


def _softplus(v):
    return jnp.maximum(v, 0.0) + jnp.log(1.0 + jnp.exp(-jnp.abs(v)))


def _ssd_chunk(refs, rows, reverse):
    if reverse:
        xa_ref, dt_ref, dtb_ref, alog_ref, yf_ref, z_ref, dskip_ref, ng_ref, o_ref, h_ref = refs
    else:
        xa_ref, dt_ref, dtb_ref, alog_ref, o_ref, h_ref = refs
    L = CHUNK
    xa = xa_ref[0, rows, :]
    dt = _softplus(dt_ref[0, rows, :] + dtb_ref[...])
    la = dt * (-jnp.exp(alog_ref[...]))
    r = lax.broadcasted_iota(jnp.int32, (L, L), 0)
    c = lax.broadcasted_iota(jnp.int32, (L, L), 1)
    keep = (r <= c) if reverse else (r >= c)
    cum = jnp.dot(keep.astype(F32), la, preferred_element_type=F32, precision=HIGHEST)
    cum_t = cum.T
    off = SSM_HEADS if reverse else 0
    tot_row = 0 if reverse else L - 1
    hpg = SSM_HEADS // SSM_GROUPS
    gw = hpg * SSM_HEAD_DIM
    lane_head = lax.broadcasted_iota(jnp.int32, (L, gw), 1) // SSM_HEAD_DIM
    ys = []
    for g in range(SSM_GROUPS):
        b_f = xa[:, SSM_INNER + g * SSM_STATE:SSM_INNER + (g + 1) * SSM_STATE]
        c_b = xa[:, SSM_INNER + (SSM_GROUPS + g) * SSM_STATE:SSM_INNER + (SSM_GROUPS + g + 1) * SSM_STATE].astype(BF16)
        gram = lax.dot_general(c_b, b_f.astype(BF16), (((1,), (1,)), ((), ())), preferred_element_type=F32)
        b_t = b_f.T.astype(BF16)

        def expand(mat):
            out = jnp.broadcast_to(mat[:, off + g * hpg + hpg - 1:off + g * hpg + hpg], (L, gw))
            for j in range(hpg - 2, -1, -1):
                out = jnp.where(lane_head == j, mat[:, off + g * hpg + j:off + g * hpg + j + 1], out)
            return out

        dt_l = expand(dt)
        cum_l = expand(cum)
        tot_l = cum_l[tot_row:tot_row + 1, :]
        xdt = xa[:, g * gw:(g + 1) * gw] * dt_l
        xdt_b = xdt.astype(BF16)
        ms = []
        for j in range(hpg):
            hh = off + g * hpg + j
            seg = cum[:, hh:hh + 1] - cum_t[hh:hh + 1, :]
            dec = jnp.exp(jnp.where(keep, seg, -jnp.inf))
            ms.append((gram * dec).astype(BF16))
        rr = jnp.dot(jnp.concatenate(ms, axis=0), xdt_b, preferred_element_type=F32)
        y = jnp.where(lane_head == 0, rr[0:L], 0.0)
        for j in range(1, hpg):
            y = y + jnp.where(lane_head == j, rr[j * L:(j + 1) * L], 0.0)
        hg = h_ref[g]
        y = y + jnp.dot(c_b, hg.astype(BF16), preferred_element_type=F32) * jnp.exp(cum_l)
        wm = jnp.exp(tot_l - cum_l)
        h_ref[g] = jnp.exp(tot_l) * hg + jnp.dot(b_t, (xdt * wm).astype(BF16), preferred_element_type=F32)
        ys.append(y)
    y = jnp.concatenate(ys, axis=1)
    if not reverse:
        o_ref[0, rows, :] = y
        return
    y = yf_ref[0, rows, :] + y + dskip_ref[...] * xa[:, :SSM_INNER]
    y = y * _silu(z_ref[0, rows, :].astype(F32))
    outs = []
    for g in range(SSM_GROUPS):
        yy = y[:, g * gw:(g + 1) * gw]
        outs.append(yy * lax.rsqrt(jnp.mean(yy * yy, axis=-1, keepdims=True) + EPS) * ng_ref[:, g * gw:(g + 1) * gw])
    o_ref[0, rows, :] = jnp.concatenate(outs, axis=1).astype(o_ref.dtype)


def _ssd_specs(xa, u, dt_raw, dt_bias, a_log, yf, d_skip, norm_g, cidx, blk, reverse):
    pad16 = lambda p: jnp.pad(p.reshape(1, -1), ((0, 0), (0, LANES - 2 * SSM_HEADS)))
    in_specs = [pl.BlockSpec((1, blk, SSM_XBC), lambda bi, s: (bi, cidx(s), 0)),
                pl.BlockSpec((1, blk, LANES), lambda bi, s: (bi, cidx(s), 0)),
                pl.BlockSpec((1, LANES), lambda bi, s: (0, 0)),
                pl.BlockSpec((1, LANES), lambda bi, s: (0, 0))]
    args = [xa, dt_raw, pad16(dt_bias), pad16(a_log)]
    if reverse:
        in_specs += [pl.BlockSpec((1, blk, SSM_INNER), lambda bi, s: (bi, cidx(s), 0)),
                     pl.BlockSpec((1, blk, SSM_INNER), lambda bi, s: (bi, cidx(s), COL_Z // SSM_INNER)),
                     pl.BlockSpec((1, SSM_INNER), lambda bi, s: (0, 0)),
                     pl.BlockSpec((1, SSM_INNER), lambda bi, s: (0, 0))]
        args += [yf, u, jnp.repeat(d_skip, SSM_HEAD_DIM).reshape(1, -1), norm_g.reshape(1, -1)]
    return in_specs, args


def _ret_rope(v, cos, sin):
    lo, hi = v[:, :LANES], v[:, LANES:]
    return jnp.concatenate([lo * cos - hi * sin, lo * sin + hi * cos], axis=1)


def _ret_tables(refs, tab_ref, dc_ref, reverse):
    L = CHUNK
    lg = -jnp.exp(refs[5][...])
    t = lax.broadcasted_iota(jnp.int32, (L, 1), 0).astype(F32)
    if reverse:
        tab_ref[:, :RET_INNER] = jnp.exp(lg * (L - t))
        tab_ref[:, RET_INNER:] = jnp.exp(lg * t)
        return
    tab_ref[:, :RET_INNER] = jnp.exp(lg * (t + 1.0))
    tab_ref[:, RET_INNER:] = jnp.exp(lg * (L - 1.0 - t))
    lgb = -jnp.exp(refs[6][...])
    dl = (lax.broadcasted_iota(jnp.int32, (L, L), 0) - lax.broadcasted_iota(jnp.int32, (L, L), 1)).astype(F32)
    for h in range(RET_HEADS):
        vs = slice(h * RET_V_DIM, (h + 1) * RET_V_DIM)
        dc_ref[h] = jnp.where(dl > 0, jnp.exp(lg[:, vs] * jnp.maximum(dl, 0.0)),
                              jnp.where(dl < 0, jnp.exp(lgb[:, vs] * jnp.maximum(-dl, 0.0)), 2.0))


def _ret_chunk(refs, tab_ref, dc_ref, rows, reverse):
    if reverse:
        (q_ref, k_ref, v_ref, cos_ref, sin_ref, dec_ref, yf_ref, g_ref, gng_ref, gnb_ref, o_ref, h_ref) = refs
    else:
        (q_ref, k_ref, v_ref, cos_ref, sin_ref, dec_ref, decb_ref, o_ref, h_ref) = refs
    L = CHUNK
    qk = RET_HEADS * RET_QK_DIM
    cos, sin = cos_ref[rows, :], sin_ref[rows, :]
    q = _ret_rope(q_ref[0, rows, :].astype(F32), cos, sin)
    k = _ret_rope(k_ref[0, rows, :].astype(F32), cos, sin) * (RET_QK_DIM ** -0.5)
    v = v_ref[0, rows, :].astype(F32)
    q_b = q.astype(BF16)
    k_t = k.T.astype(BF16)
    lg = -jnp.exp(dec_ref[...])
    e_in = tab_ref[:, :RET_INNER]
    w_st = tab_ref[:, RET_INNER:]
    hs = h_ref[...]
    y = jnp.dot(q_b, hs.astype(BF16), preferred_element_type=F32) * e_in
    upd = jnp.dot(k_t, (v * w_st).astype(BF16), preferred_element_type=F32)
    row_head = (lax.broadcasted_iota(jnp.int32, (qk, RET_INNER), 0) % LANES) // (RET_QK_DIM // 2)
    lane_head = lax.broadcasted_iota(jnp.int32, (qk, RET_INNER), 1) // RET_V_DIM
    h_ref[...] = jnp.exp(lg * float(L)) * hs + jnp.where(row_head == lane_head, upd, 0.0)

    if not reverse:
        q_head = (lax.broadcasted_iota(jnp.int32, (L, qk), 1) % LANES) // (RET_QK_DIM // 2)
        qs = jnp.concatenate([jnp.where(q_head == h, q, 0.0) for h in range(RET_HEADS)], axis=0).astype(BF16)
        sc = jnp.dot(qs, k_t, preferred_element_type=F32)
        parts = []
        for h in range(RET_HEADS):
            vs = slice(h * RET_V_DIM, (h + 1) * RET_V_DIM)
            p = (sc[h * L:(h + 1) * L] * dc_ref[h]).astype(BF16)
            parts.append(jnp.dot(p, v[:, vs].astype(BF16), preferred_element_type=F32))
        o_ref[0, rows, :] = y + jnp.concatenate(parts, axis=1)
        return
    y = yf_ref[0, rows, :] + y
    outs = []
    for h in range(RET_HEADS):
        vs = slice(h * RET_V_DIM, (h + 1) * RET_V_DIM)
        yy = y[:, vs]
        mu = jnp.mean(yy, axis=-1, keepdims=True)
        var = jnp.mean(jnp.square(yy - mu), axis=-1, keepdims=True)
        outs.append((yy - mu) * lax.rsqrt(var + EPS) * gng_ref[:, vs] + gnb_ref[:, vs])
    o_ref[0, rows, :] = (_silu(g_ref[0, rows, :].astype(F32)) * jnp.concatenate(outs, axis=1)).astype(o_ref.dtype)


def _ret_specs(u, cos4, sin4, ret_decay, yf, gn_g, gn_b, cidx, blk, reverse):
    qk = RET_HEADS * RET_QK_DIM
    lane_dec = lambda d: jnp.repeat(d, RET_V_DIM).reshape(1, -1)
    in_specs = [pl.BlockSpec((1, blk, qk), lambda bi, s: (bi, cidx(s), COL_RQ // qk)),
                pl.BlockSpec((1, blk, qk), lambda bi, s: (bi, cidx(s), COL_RK // qk)),
                pl.BlockSpec((1, blk, RET_INNER), lambda bi, s: (bi, cidx(s), COL_RV // RET_INNER)),
                pl.BlockSpec((blk, LANES), lambda bi, s: (cidx(s), 0)),
                pl.BlockSpec((blk, LANES), lambda bi, s: (cidx(s), 0)),
                pl.BlockSpec((1, RET_INNER), lambda bi, s: (0, 0))]
    args = [u, u, u, cos4, sin4, lane_dec(ret_decay[1] if reverse else ret_decay[0])]
    if reverse:
        in_specs += [pl.BlockSpec((1, blk, RET_INNER), lambda bi, s: (bi, cidx(s), 0)),
                     pl.BlockSpec((1, blk, RET_INNER), lambda bi, s: (bi, cidx(s), COL_RG // RET_INNER)),
                     pl.BlockSpec((1, RET_INNER), lambda bi, s: (0, 0)),
                     pl.BlockSpec((1, RET_INNER), lambda bi, s: (0, 0))]
        args += [yf, u, gn_g.reshape(1, -1), gn_b.reshape(1, -1)]
    else:
        in_specs += [pl.BlockSpec((1, RET_INNER), lambda bi, s: (0, 0))]
        args += [lane_dec(ret_decay[1])]
    return in_specs, args


def _scan_kernel(*refs, n_ssd_in, reverse):
    n_in = len(refs) - 6
    ssd_o, ret_o, ssd_h, ret_h, ret_tab, ret_dc = refs[n_in:]
    ssd_refs = refs[:n_ssd_in] + (ssd_o, ssd_h)
    ret_refs = refs[n_ssd_in:n_in] + (ret_o, ret_h)

    @pl.when(pl.program_id(1) == 0)
    def _():
        ssd_h[...] = jnp.zeros_like(ssd_h)
        ret_h[...] = jnp.zeros_like(ret_h)
        _ret_tables(ret_refs, ret_tab, ret_dc, reverse)

    order = range(SCAN_CHUNKS - 1, -1, -1) if reverse else range(SCAN_CHUNKS)
    for ci in order:
        rows = slice(ci * CHUNK, (ci + 1) * CHUNK)
        _ssd_chunk(ssd_refs, rows, reverse)
        _ret_chunk(ret_refs, ret_tab, ret_dc, rows, reverse)


def _scan(ssd_in, ret_in, n_ctx, reverse):
    b, nt, _ = ssd_in[0].shape
    blk = SCAN_CHUNKS * CHUNK
    cidx = functools.partial(_chunk_index, reverse=reverse, n_chunks=nt // blk, n_ctx_chunks=n_ctx // blk)
    ssd_specs, ssd_args = _ssd_specs(*ssd_in, cidx, blk, reverse)
    ret_specs, ret_args = _ret_specs(*ret_in, cidx, blk, reverse)
    out_spec = lambda w: pl.BlockSpec((1, blk, w), lambda bi, s: (bi, cidx(s), 0))
    out_dtype = BF16 if reverse else F32
    return pl.pallas_call(
        functools.partial(_scan_kernel, n_ssd_in=len(ssd_args), reverse=reverse),
        out_shape=(jax.ShapeDtypeStruct((b, nt, SSM_INNER), out_dtype), jax.ShapeDtypeStruct((b, nt, RET_INNER), out_dtype)),
        grid=(b, nt // blk),
        in_specs=ssd_specs + ret_specs,
        out_specs=(out_spec(SSM_INNER), out_spec(RET_INNER)),
        scratch_shapes=[pltpu.VMEM((SSM_GROUPS, SSM_STATE, SSM_INNER // SSM_GROUPS), F32),
                        pltpu.VMEM((RET_HEADS * RET_QK_DIM, RET_INNER), F32),
                        pltpu.VMEM((CHUNK, 2 * RET_INNER), F32),
                        pltpu.VMEM((RET_HEADS, CHUNK, CHUNK), F32)],
        compiler_params=_cparams(("parallel", "arbitrary")),
    )(*ssd_args, *ret_args)


def _mla_rope(v, c, s1, s2):
    return v * c + pltpu.roll(v, 16, 1) * s1 + pltpu.roll(v, LANES - 16, 1) * s2


def _rms(v, g):
    return v * lax.rsqrt(jnp.mean(v * v, axis=-1, keepdims=True) + EPS) * g


def _mlaprep_kernel(cq_ref, ckv_ref, kr_ref, qg_ref, kvg_ref, wq_ref, wk_ref, wv_ref, c_ref, s1_ref, s2_ref,
                    qt_out, k_out, vt_out):
    c, s1, s2 = c_ref[...], s1_ref[...], s2_ref[...]
    scale = (MLA_NOPE + MLA_ROPE) ** -0.5 * math.log2(math.e)
    q = jnp.dot(_rms(cq_ref[0].astype(F32), qg_ref[...]).astype(BF16), wq_ref[...], preferred_element_type=F32)
    ckv = _rms(ckv_ref[0].astype(F32), kvg_ref[...]).astype(BF16)
    kn = jnp.dot(ckv, wk_ref[...], preferred_element_type=F32)
    vt_out[0] = jnp.dot(ckv, wv_ref[...], preferred_element_type=F32).T.astype(BF16)
    krr = _mla_rope(kr_ref[0].astype(F32), c, s1, s2)
    for h in range(MLA_HEADS):
        hs = slice(h * LANES, (h + 1) * LANES)
        qt_out[0, hs, :] = (_mla_rope(q[:, hs], c, s1, s2) * scale).T.astype(BF16)
        k_out[0, :, hs] = (kn[:, hs] + krr).astype(BF16)


def _mlaprep(u, q_g, kv_g, wq, wk, wv, ctab, s1tab, s2tab):
    b, nt, _ = u.shape
    tm = _tile(nt, 1280, 256)
    hw = MLA_HEADS * LANES
    const = lambda shape: pl.BlockSpec(shape, lambda bi, i: (0, 0))
    return pl.pallas_call(
        _mlaprep_kernel,
        out_shape=(jax.ShapeDtypeStruct((b, hw, nt), BF16), jax.ShapeDtypeStruct((b, nt, hw), BF16),
                   jax.ShapeDtypeStruct((b, MLA_HEADS * MLA_V, nt), BF16)),
        grid=(b, nt // tm),
        in_specs=[pl.BlockSpec((1, tm, MLA_Q_RANK), lambda bi, i: (bi, i, COL_CQ // MLA_Q_RANK)),
                  pl.BlockSpec((1, tm, MLA_KV_RANK), lambda bi, i: (bi, i, COL_CKV // MLA_KV_RANK)),
                  pl.BlockSpec((1, tm, LANES), lambda bi, i: (bi, i, COL_KR // LANES)),
                  const((1, MLA_Q_RANK)), const((1, MLA_KV_RANK)),
                  const((MLA_Q_RANK, hw)), const((MLA_KV_RANK, hw)), const((MLA_KV_RANK, MLA_HEADS * MLA_V)),
                  pl.BlockSpec((tm, LANES), lambda bi, i: (i, 0)),
                  pl.BlockSpec((tm, LANES), lambda bi, i: (i, 0)),
                  pl.BlockSpec((tm, LANES), lambda bi, i: (i, 0))],
        out_specs=(pl.BlockSpec((1, hw, tm), lambda bi, i: (bi, 0, i)),
                   pl.BlockSpec((1, tm, hw), lambda bi, i: (bi, i, 0)),
                   pl.BlockSpec((1, MLA_HEADS * MLA_V, tm), lambda bi, i: (bi, 0, i))),
        compiler_params=_cparams(("parallel", "parallel")),
    )(u, u, u, q_g.reshape(1, -1), kv_g.reshape(1, -1), wq, wk, wv, ctab, s1tab, s2tab)


ACC_ROWS = MLA_V + 16


def _flash_kernel(k_ref, qt_ref, vt_ref, o_ref, m_ref, off_ref, acc_ref, *, tq, tk, n_ctx, nk):
    i = pl.program_id(2)
    j = pl.program_id(3)

    @pl.when(j == 0)
    def _():
        m_ref[...] = jnp.full_like(m_ref, -jnp.inf)
        acc_ref[...] = jnp.zeros_like(acc_ref)
        for hh in range(2):
            hs = slice(hh * LANES, (hh + 1) * LANES)
            s0 = jnp.dot(k_ref[0, :LANES, hs], qt_ref[0, hs, :], preferred_element_type=F32)
            off_ref[hh] = jnp.max(s0, axis=0, keepdims=True)

    def step(masked):
        ones = jnp.ones((ACC_ROWS - MLA_V, tk), BF16)
        for hh in range(2):
            hs = slice(hh * LANES, (hh + 1) * LANES)
            s = jnp.dot(k_ref[0, :, hs], qt_ref[0, hs, :], preferred_element_type=F32)
            m_prev = m_ref[hh]
            off = jnp.where(m_prev == -jnp.inf, off_ref[hh], m_prev)
            s = (s - off).astype(BF16)
            if masked:
                keys = j * tk + lax.broadcasted_iota(jnp.int32, (tk, n_ctx), 0)
                s_ctx = jnp.where(keys >= n_ctx, -3e38, s[:, :n_ctx])
                s = s_ctx if tq == n_ctx else jnp.concatenate([s_ctx, s[:, n_ctx:]], axis=1)
            rel = jnp.maximum(jnp.max(s, axis=0, keepdims=True).astype(F32), m_prev - off)
            m_new = rel + off
            alpha = jnp.exp2(m_prev - m_new)
            p = jnp.exp2(s - rel.astype(BF16))
            vs = slice(hh * MLA_V, (hh + 1) * MLA_V)
            lhs = jnp.concatenate([vt_ref[0, vs, :], ones], axis=0)
            acc_ref[hh] = alpha * acc_ref[hh] + jnp.dot(lhs, p, preferred_element_type=F32)
            m_ref[hh] = m_new

    @pl.when(i * tq < n_ctx)
    def _():
        step(True)

    @pl.when(i * tq >= n_ctx)
    def _():
        step(False)

    @pl.when(j == nk - 1)
    def _():
        outs = []
        for hh in range(2):
            a = acc_ref[hh]
            outs.append(a[:MLA_V] * (1.0 / a[MLA_V:MLA_V + 1]))
        o_ref[0] = jnp.concatenate(outs, axis=0).T.astype(o_ref.dtype)


def _flash(k, qt, vt, n_ctx, tq_target=3328, tk_target=1280):
    b, nt, _ = k.shape
    tq = _tile(nt, tq_target, LANES)
    tk = _tile(nt, tk_target, LANES)
    assert tq >= n_ctx and n_ctx % LANES == 0
    nk = nt // tk
    pairs = MLA_HEADS // 2
    return pl.pallas_call(
        functools.partial(_flash_kernel, tq=tq, tk=tk, n_ctx=n_ctx, nk=nk),
        out_shape=jax.ShapeDtypeStruct((b, nt, MLA_HEADS * MLA_V), BF16),
        grid=(b, pairs, nt // tq, nk),
        in_specs=[pl.BlockSpec((1, tk, 2 * LANES), lambda bi, p, i, j: (bi, j, p)),
                  pl.BlockSpec((1, 2 * LANES, tq), lambda bi, p, i, j: (bi, p, i)),
                  pl.BlockSpec((1, 2 * MLA_V, tk), lambda bi, p, i, j: (bi, p, j))],
        out_specs=pl.BlockSpec((1, tq, 2 * MLA_V), lambda bi, p, i, j: (bi, i, p)),
        scratch_shapes=[pltpu.VMEM((2, 1, tq), F32), pltpu.VMEM((2, 1, tq), F32), pltpu.VMEM((2, ACC_ROWS, tq), F32)],
        compiler_params=_cparams(("parallel", "parallel", "parallel", "arbitrary")),
    )(k, qt, vt)


def _merge_kernel(xs_ref, gl_ref, b0_ref, b1_ref, b2_ref, b3_ref, wb_ref, wo_ref, mod_ref, o_ref, *, tm, n_ctx):
    i = pl.program_id(1)
    merged = None
    for n, br in enumerate((b0_ref, b1_ref, b2_ref, b3_ref)):
        proj = jnp.dot(br[0], wb_ref[n], preferred_element_type=F32)
        term = jax.nn.sigmoid(gl_ref[0, :, n * D_MODEL:(n + 1) * D_MODEL].astype(F32)) * proj
        merged = term if merged is None else merged + term
    out = jnp.dot(merged.astype(BF16), wo_ref[...], preferred_element_type=F32)
    o_ref[0] = xs_ref[0] + _res_gate(mod_ref[0], i * tm, tm, n_ctx, 0) * out


def _merge(xs, u, branches, wb, wo, modtab, n_ctx):
    b, nt, d = xs.shape
    tm = _tile(nt, 640, LANES)
    row = lambda w: pl.BlockSpec((1, tm, w), lambda bi, i: (bi, i, 0))
    return pl.pallas_call(
        functools.partial(_merge_kernel, tm=tm, n_ctx=n_ctx),
        out_shape=jax.ShapeDtypeStruct((b, nt, d), F32),
        grid=(b, nt // tm),
        in_specs=[row(d), row(N_BRANCH * d), row(BRANCH_DIM), row(BRANCH_DIM), row(BRANCH_DIM), row(BRANCH_DIM),
                  pl.BlockSpec((N_BRANCH, BRANCH_DIM, d), lambda bi, i: (0, 0, 0)),
                  pl.BlockSpec((d, d), lambda bi, i: (0, 0)),
                  pl.BlockSpec((1, 16, d), lambda bi, i: (bi, 0, 0))],
        out_specs=row(d),
        compiler_params=_cparams(("parallel", "parallel")),
    )(xs, u, *branches, wb, wo, modtab)


def _ffn_kernel(xs_ref, mod_ref, g_ref, wa_ref, wg_ref, wo_ref, o_ref, h_ref, acc_ref, *, tm, n_ctx, nj):
    i = pl.program_id(1)
    j = pl.program_id(2)

    @pl.when(j == 0)
    def _():
        h_ref[...] = _norm_mod(xs_ref[0], mod_ref[0], g_ref[...], i * tm, n_ctx, 1).astype(BF16)

    h = h_ref[...]
    a = jnp.dot(h, wa_ref[...], preferred_element_type=F32)
    gate = jnp.dot(h, wg_ref[...], preferred_element_type=F32)
    part = jnp.dot((_silu(gate) * a).astype(BF16), wo_ref[...], preferred_element_type=F32)

    @pl.when(j == 0)
    def _():
        acc_ref[...] = part

    @pl.when(j > 0)
    def _():
        acc_ref[...] += part

    @pl.when(j == nj - 1)
    def _():
        o_ref[0] = xs_ref[0] + _res_gate(mod_ref[0], i * tm, tm, n_ctx, 1) * acc_ref[...]


def _ffn(xs, modtab, g, w_in, w_out, n_ctx):
    b, nt, d = xs.shape
    f = w_out.shape[0]
    tm = _tile(nt, 640, 128)
    tf = _tile(f, 1408, LANES)
    nj = f // tf
    return pl.pallas_call(
        functools.partial(_ffn_kernel, tm=tm, n_ctx=n_ctx, nj=nj),
        out_shape=jax.ShapeDtypeStruct((b, nt, d), F32),
        grid=(b, nt // tm, nj),
        in_specs=[pl.BlockSpec((1, tm, d), lambda bi, i, j: (bi, i, 0)),
                  pl.BlockSpec((1, 16, d), lambda bi, i, j: (bi, 0, 0)),
                  pl.BlockSpec((1, d), lambda bi, i, j: (0, 0)),
                  pl.BlockSpec((d, tf), lambda bi, i, j: (0, j)),
                  pl.BlockSpec((d, tf), lambda bi, i, j: (0, nj + j)),
                  pl.BlockSpec((tf, d), lambda bi, i, j: (j, 0))],
        out_specs=pl.BlockSpec((1, tm, d), lambda bi, i, j: (bi, i, 0)),
        scratch_shapes=[pltpu.VMEM((tm, d), BF16), pltpu.VMEM((tm, d), F32)],
        compiler_params=_cparams(("parallel", "parallel", "arbitrary")),
    )(xs, modtab, g.reshape(1, d), w_in, w_in, w_out)


def _final_kernel(x_ref, g_ref, o_ref):
    o_ref[0] = _rms(x_ref[0], g_ref[...])


def _final_norm(xs, g, n_ctx):
    b, nt, d = xs.shape
    t = 256
    skip = n_ctx // t
    return pl.pallas_call(
        _final_kernel,
        out_shape=jax.ShapeDtypeStruct((b, nt - n_ctx, d), F32),
        grid=(b, (nt - n_ctx) // t),
        in_specs=[pl.BlockSpec((1, t, d), lambda bi, i: (bi, i + skip, 0)),
                  pl.BlockSpec((1, d), lambda bi, i: (0, 0))],
        out_specs=pl.BlockSpec((1, t, d), lambda bi, i: (bi, i, 0)),
        compiler_params=_cparams(("parallel", "parallel")),
    )(xs, g.reshape(1, d))


def _prep_w_in(w):
    d = w.shape[0]
    o_conv, o_z, o_xbc, o_dt = 0, 1024, 1536, 2560
    o_rq, o_rk, o_rv, o_rg = 2576, 2832, 3088, 3600
    o_cq, o_ckv, o_kr, o_gl = 4112, 4496, 4752, 4784
    seg = lambda o, n: w[:, o:o + n]

    def halves(o):
        s = seg(o, RET_HEADS * RET_QK_DIM).reshape(d, RET_HEADS, 2, RET_QK_DIM // 2)
        return jnp.transpose(s, (0, 2, 1, 3)).reshape(d, RET_HEADS * RET_QK_DIM)

    zeros = lambda n: jnp.zeros((d, n), w.dtype)
    parts = [seg(o_gl, N_BRANCH * D_MODEL), seg(o_conv, 2 * CONV_DIM), seg(o_xbc, SSM_XBC), seg(o_z, SSM_INNER),
             seg(o_rv, RET_INNER), seg(o_rg, RET_INNER), halves(o_rq), halves(o_rk),
             seg(o_ckv, MLA_KV_RANK), seg(o_cq, MLA_Q_RANK),
             seg(o_dt, 2 * SSM_HEADS), zeros(LANES - 2 * SSM_HEADS),
             zeros(MLA_NOPE), seg(o_kr, MLA_ROPE), zeros(LANES - MLA_NOPE - MLA_ROPE),
             zeros(LANES)]
    out = jnp.concatenate(parts, axis=1).astype(BF16)
    assert out.shape[1] == N_IN_PAD
    return out


def _prep_mla_w(w_uq, w_ukv):
    rq, rkv = w_uq.shape[0], w_ukv.shape[0]
    hd = MLA_NOPE + MLA_ROPE
    wq = jnp.pad(w_uq.reshape(rq, MLA_HEADS, hd), ((0, 0), (0, 0), (0, LANES - hd))).reshape(rq, MLA_HEADS * LANES)
    kv = w_ukv.reshape(rkv, MLA_HEADS, MLA_NOPE + MLA_V)
    wk = jnp.pad(kv[:, :, :MLA_NOPE], ((0, 0), (0, 0), (0, LANES - MLA_NOPE))).reshape(rkv, MLA_HEADS * LANES)
    wv = kv[:, :, MLA_NOPE:].reshape(rkv, MLA_HEADS * MLA_V)
    return wq.astype(BF16), wk.astype(BF16), wv.astype(BF16)


def _axial_angles(n, rot_dim):
    rows = n // GRID_W
    row = jnp.repeat(jnp.arange(rows, dtype=F32), GRID_W)
    col = jnp.tile(jnp.arange(GRID_W, dtype=F32), rows)
    nf = rot_dim // 4
    inv = ROPE_BASE ** (-jnp.arange(nf, dtype=F32) / nf)
    return jnp.concatenate([row[:, None] * inv, col[:, None] * inv], axis=-1)


def _rope_tables(n_lat, n_ctx):
    ang = _axial_angles(n_lat, RET_QK_DIM)
    cos4 = jnp.tile(jnp.cos(ang), (1, RET_HEADS))
    sin4 = jnp.tile(jnp.sin(ang), (1, RET_HEADS))
    cos4 = jnp.concatenate([jnp.ones((n_ctx, LANES), F32), cos4], axis=0)
    sin4 = jnp.concatenate([jnp.zeros((n_ctx, LANES), F32), sin4], axis=0)
    ang = _axial_angles(n_lat, MLA_ROPE)
    c, s = jnp.cos(ang), jnp.sin(ang)
    half = MLA_ROPE // 2
    one = jnp.ones((n_lat, MLA_NOPE), F32)
    zero = lambda w: jnp.zeros((n_lat, w), F32)
    tail = LANES - MLA_NOPE - MLA_ROPE
    ctab = jnp.concatenate([one, c, c, jnp.ones((n_lat, tail), F32)], axis=1)
    s1 = jnp.concatenate([zero(MLA_NOPE + half), s, zero(tail)], axis=1)
    s2 = jnp.concatenate([zero(MLA_NOPE), -s, zero(half + tail)], axis=1)
    ctab = jnp.concatenate([jnp.ones((n_ctx, LANES), F32), ctab], axis=0)
    s1 = jnp.concatenate([jnp.zeros((n_ctx, LANES), F32), s1], axis=0)
    s2 = jnp.concatenate([jnp.zeros((n_ctx, LANES), F32), s2], axis=0)
    return cos4, sin4, ctab, s1, s2


def kernel(x, c, ctx, c_ctx, w_ada, b_ada, norm1_g, norm2_g, w_in, conv_w, conv_b, conv_ln_g, conv_ln_b,
           ssm_conv_w, ssm_conv_b, ssm_dt_bias, ssm_a_log, ssm_d, ssm_norm_g, ret_decay, ret_gn_g, ret_gn_b,
           mla_q_norm_g, mla_kv_norm_g, mla_w_uq, mla_w_ukv, w_branch, w_out, w_ffn_in, w_ffn_out, final_norm_g):
    batch, n_lat, d = x.shape
    n_ctx = ctx.shape[1]
    depth = w_in.shape[0]
    assert d == D_MODEL and n_ctx % 256 == 0 and n_lat % 256 == 0 and batch + 1 <= 8
    assert n_ctx % (SCAN_CHUNKS * CHUNK) == 0
    cos4, sin4, ctab, s1tab, s2tab = _rope_tables(n_lat, n_ctx)
    xs = jnp.concatenate([ctx, x], axis=1)
    cvec = jnp.zeros((8, d), F32).at[:batch].set(c).at[batch].set(c_ctx)
    for i in range(depth):
        mod = _ada(cvec, w_ada[i], b_ada[i]).reshape(8, 6, d)
        modtab = jnp.zeros((batch, 16, d), F32)
        modtab = modtab.at[:, 0:6].set(jnp.broadcast_to(mod[batch], (batch, 6, d))).at[:, 8:14].set(mod[:batch])
        u, dt_raw = _inproj(xs, modtab, norm1_g[i], _prep_w_in(w_in[i]), n_ctx)
        conv_y = _convmod(u, conv_w[i], conv_b[i], conv_ln_g[i], conv_ln_b[i], n_ctx)
        xa = _ssmconv(u, ssm_conv_w[i], ssm_conv_b[i], n_ctx)
        ssm_f, ret_f = _scan((xa, u, dt_raw, ssm_dt_bias[i], ssm_a_log[i], None, None, None),
                             (u, cos4, sin4, ret_decay[i], None, None, None), n_ctx, False)
        ssm_y, ret_y = _scan((xa, u, dt_raw, ssm_dt_bias[i], ssm_a_log[i], ssm_f, ssm_d[i], ssm_norm_g[i]),
                             (u, cos4, sin4, ret_decay[i], ret_f, ret_gn_g[i], ret_gn_b[i]), n_ctx, True)
        wq, wk, wv = _prep_mla_w(mla_w_uq[i], mla_w_ukv[i])
        qt, k, vt = _mlaprep(u, mla_q_norm_g[i], mla_kv_norm_g[i], wq, wk, wv, ctab, s1tab, s2tab)
        att = _flash(k, qt, vt, n_ctx)
        xs = _merge(xs, u, (conv_y, ssm_y, ret_y, att), w_branch[i].astype(BF16), w_out[i].astype(BF16),
                    modtab, n_ctx)
        xs = _ffn(xs, modtab, norm2_g[i], w_ffn_in[i].astype(BF16), w_ffn_out[i].astype(BF16), n_ctx)
    return _final_norm(xs, final_norm_g, n_ctx)
```
